```python
import jax, jax.numpy as jnp
from jax import lax
import numpy as np

D_MODEL = 2048
BATCH = 4
SEQ = 2048
DEPTH = 4

N_A_LAYERS = DEPTH // 2
N_B_LAYERS = DEPTH - N_A_LAYERS
RWKV_HEAD = 64
RWKV_HEADS = D_MODEL // RWKV_HEAD
D_DECAY_LORA = max(32, int(round(1.8 * D_MODEL ** 0.5 / 32)) * 32)
D_AAA_LORA = max(32, int(round(1.8 * D_MODEL ** 0.5 / 32)) * 32)
D_MV_LORA = max(32, int(round(1.3 * D_MODEL ** 0.5 / 32)) * 32)
D_GATE_LORA = max(32, int(round(0.6 * D_MODEL ** 0.8 / 32)) * 32)
GN_EPS = 64e-5
ATT_HEAD = 128
Q_HEADS = D_MODEL // ATT_HEAD
KV_HEADS = 4
GROUP = Q_HEADS // KV_HEADS
MOBA_BLOCK = 256
MOBA_TOPK = 3
Q_CHUNK = 16
ROPE_THETA = 10000.0
D_FF = 4 * D_MODEL
RMS_EPS = 1e-6

kernel_name = 'yoco_rwkv7_moba_hybrid'


def rms_norm(x, g):
    xf = x.astype(jnp.float32)
    y = xf * lax.rsqrt(jnp.mean(xf * xf, axis=-1, keepdims=True) + RMS_EPS)
    return (y * g.astype(jnp.float32)).astype(x.dtype)


def rope(x, positions):
    half = x.shape[-1] // 2
    inv = ROPE_THETA ** (-jnp.arange(half, dtype=jnp.float32) / half)
    ang = positions.astype(jnp.float32)[:, None] * inv[None, :]
    cos = jnp.cos(ang)[None, :, None, :]
    sin = jnp.sin(ang)[None, :, None, :]
    xf = x.astype(jnp.float32)
    x1, x2 = xf[..., :half], xf[..., half:]
    return jnp.concatenate([x1 * cos - x2 * sin, x2 * cos + x1 * sin], axis=-1).astype(x.dtype)


def sq_relu_mlp(x, w1, w2):
    return jnp.square(jax.nn.relu(x @ w1)) @ w2


def wkv7_scan(r, decay, k, v, a, b):
    B, T, H, N = r.shape

    def step(S, inp):
        r_t, w_t, k_t, v_t, a_t, b_t = inp
        sa = jnp.einsum('bhij,bhj->bhi', S, a_t)
        S = (S * w_t[:, :, None, :] + sa[..., None] * b_t[:, :, None, :]
             + v_t[..., None] * k_t[:, :, None, :])
        y_t = jnp.einsum('bhij,bhj->bhi', S, r_t)
        return S, y_t

    xs = tuple(jnp.moveaxis(z, 1, 0) for z in (r, decay, k, v, a, b))
    S0 = jnp.zeros((B, H, N, N), jnp.float32)
    _, y = lax.scan(step, S0, xs)
    return jnp.moveaxis(y, 0, 1)


def rwkv7_time_mix(x, v_first, mu, w_rkv, w0, w1, w2, a0, a1, a2, g1, g2,
                   k_k, k_a, r_k, gn_w, gn_b, w_o, v_lora):
    B, T, D = x.shape
    H, N = RWKV_HEADS, RWKV_HEAD
    xx = jnp.pad(x, ((0, 0), (1, 0), (0, 0)))[:, :-1] - x
    xr, xw, xk, xv, xa, xg = [x + xx * mu[i] for i in range(6)]
    rkv = jnp.einsum('nbtd,nde->nbte', jnp.stack([xr, xk, xv]), w_rkv)
    r, k, v = rkv[0], rkv[1], rkv[2]
    w = -jax.nn.softplus(-(w0 + jnp.tanh(xw @ w1) @ w2)) - 0.5
    if v_lora is None:
        v_first = v
    else:
        v0, v1, v2 = v_lora
        v = v + (v_first - v) * jax.nn.sigmoid(v0 + (xv @ v1) @ v2)
    a = jax.nn.sigmoid(a0 + (xa @ a1) @ a2)
    g = jax.nn.sigmoid(xg @ g1) @ g2
    f32 = jnp.float32
    kk = (k * k_k).astype(f32).reshape(B, T, H, N)
    kk = kk / jnp.maximum(jnp.sqrt(jnp.sum(kk * kk, axis=-1, keepdims=True)), 1e-12)
    k = k * (1.0 + (a - 1.0) * k_a)
    rh = r.astype(f32).reshape(B, T, H, N)
    kh = k.astype(f32).reshape(B, T, H, N)
    vh = v.astype(f32).reshape(B, T, H, N)
    ah = a.astype(f32).reshape(B, T, H, N)
    decay = jnp.exp(-jnp.exp(w.astype(f32))).reshape(B, T, H, N)
    y = wkv7_scan(rh, decay, kh, vh, -kk, kk * ah)
    mean = jnp.mean(y, axis=-1, keepdims=True)
    var = jnp.mean(jnp.square(y - mean), axis=-1, keepdims=True)
    y = ((y - mean) * lax.rsqrt(var + GN_EPS)).reshape(B, T, D)
    y = y * gn_w.astype(f32) + gn_b.astype(f32)
    bonus = jnp.sum(rh * kh * r_k.astype(f32), axis=-1, keepdims=True) * vh
    y = y + bonus.reshape(B, T, D)
    out = (y.astype(x.dtype) * g) @ w_o
    return out, v_first


def shared_kv(h, g_kv, w_kv):
    B, T, _ = h.shape
    hn = rms_norm(h, g_kv)
    kv = hn @ w_kv
    k, v = jnp.split(kv, 2, axis=-1)
    k = rope(k.reshape(B, T, KV_HEADS, ATT_HEAD), jnp.arange(T))
    v = v.reshape(B, T, KV_HEADS, ATT_HEAD)
    nb = -(-T // MOBA_BLOCK)
    pad = nb * MOBA_BLOCK - T
    k_blk = jnp.pad(k, ((0, 0), (0, pad), (0, 0), (0, 0))).reshape(
        B, nb, MOBA_BLOCK, KV_HEADS, ATT_HEAD).transpose(0, 3, 1, 2, 4)
    v_blk = jnp.pad(v, ((0, 0), (0, pad), (0, 0), (0, 0))).reshape(
        B, nb, MOBA_BLOCK, KV_HEADS, ATT_HEAD).transpose(0, 3, 1, 2, 4)
    k_mean = jnp.mean(k_blk.astype(jnp.float32), axis=3)
    return k_blk, v_blk, k_mean


def moba_attention(q, k_blk, v_blk, k_mean):
    B, T, _, hd = q.shape
    nb = k_blk.shape[2]
    topk = min(MOBA_TOPK, nb)
    scale = hd ** -0.5
    f32 = jnp.float32
    qg = q.reshape(B, T, KV_HEADS, GROUP, hd).transpose(0, 2, 3, 1, 4)
    q_blk = jnp.arange(T) // MOBA_BLOCK
    gate = jnp.einsum('bkgtd,bknd->bkgtn', qg.astype(f32), k_mean)
    past = jnp.arange(nb)[None, :] < q_blk[:, None]
    gate = jnp.where(past, gate, -jnp.inf)
    _, sel = lax.top_k(gate, topk)
    slot_ok = jnp.arange(topk)[None, :] < q_blk[:, None]

    tp = -(-T // Q_CHUNK) * Q_CHUNK
    pad = tp - T
    nc = tp // Q_CHUNK
    qg = jnp.pad(qg, ((0, 0), (0, 0), (0, 0), (0, pad), (0, 0)))
    sel = jnp.pad(sel, ((0, 0), (0, 0), (0, 0), (0, pad), (0, 0)))
    slot_ok = jnp.pad(slot_ok, ((0, pad), (0, 0)))
    q_c = qg.reshape(B, KV_HEADS, GROUP, nc, Q_CHUNK, hd).transpose(3, 0, 1, 2, 4, 5)
    sel_c = sel.reshape(B, KV_HEADS, GROUP, nc, Q_CHUNK, topk).transpose(3, 0, 1, 2, 4, 5)
    ok_c = slot_ok.reshape(nc, Q_CHUNK, topk)
    starts = jnp.arange(nc) * Q_CHUNK
    k_flat = k_blk.reshape(B, KV_HEADS, nb * MOBA_BLOCK, hd)
    v_flat = v_blk.reshape(B, KV_HEADS, nb * MOBA_BLOCK, hd)
    bi = jnp.arange(B)[:, None, None, None, None]
    hi = jnp.arange(KV_HEADS)[None, :, None, None, None]

    def chunk(args):
        qc, sc, okc, s0 = args
        kg = k_blk[bi, hi, sc]
        vg = v_blk[bi, hi, sc]
        own0 = (s0 // MOBA_BLOCK) * MOBA_BLOCK
        k_own = lax.dynamic_slice_in_dim(k_flat, own0, MOBA_BLOCK, axis=2)
        v_own = lax.dynamic_slice_in_dim(v_flat, own0, MOBA_BLOCK, axis=2)
        qk = qc.astype(kg.dtype)
        l_sel = jnp.einsum('bkgcd,bkgcsnd->bkgcsn', qk, kg, preferred_element_type=f32) * scale
        l_sel = jnp.where(okc[None, None, None, :, :, None], l_sel, -jnp.inf)
        l_own = jnp.einsum('bkgcd,bknd->bkgcn', qk, k_own, preferred_element_type=f32) * scale
        qpos = s0 + jnp.arange(Q_CHUNK)
        kpos = own0 + jnp.arange(MOBA_BLOCK)
        l_own = jnp.where(kpos[None, :] <= qpos[:, None], l_own, -jnp.inf)
        nsel = topk * MOBA_BLOCK
        logits = jnp.concatenate([l_sel.reshape(B, KV_HEADS, GROUP, Q_CHUNK, nsel), l_own], axis=-1)
        p = jax.nn.softmax(logits, axis=-1)
        p_sel = p[..., :nsel].reshape(B, KV_HEADS, GROUP, Q_CHUNK, topk, MOBA_BLOCK)
        p_own = p[..., nsel:]
        o = (jnp.einsum('bkgcsn,bkgcsnd->bkgcd', p_sel.astype(vg.dtype), vg, preferred_element_type=f32)
             + jnp.einsum('bkgcn,bknd->bkgcd', p_own.astype(v_own.dtype), v_own, preferred_element_type=f32))
        return o.astype(qc.dtype)

    out = lax.map(chunk, (q_c, sel_c, ok_c, starts))
    out = out.transpose(1, 0, 4, 2, 3, 5).reshape(B, tp, Q_HEADS * hd)
    return out[:, :T]


def moba_layer(hn, k_blk, v_blk, k_mean, w_q, w_o):
    B, T, _ = hn.shape
    q = rope((hn @ w_q).reshape(B, T, Q_HEADS, ATT_HEAD), jnp.arange(T))
    return moba_attention(q, k_blk, v_blk, k_mean) @ w_o


def setup_inputs(seed: int = 0) -> dict:
    key = jax.random.key(seed)
    ks = iter(jax.random.split(key, 48))
    f32 = jnp.float32
    D, NA, NBL = D_MODEL, N_A_LAYERS, N_B_LAYERS
    NV = max(NA - 1, 0)

    def nrm(shape, scale):
        return jax.random.normal(next(ks), shape, f32) * scale

    def gain(shape):
        return 1.0 + 0.02 * jax.random.normal(next(ks), shape, f32)

    return {
        'x': nrm((BATCH, SEQ, D), 1.0),
        'ln_mix_g': gain((DEPTH, D)),
        'ln_ffn_g': gain((DEPTH, D)),
        'w_ff1': nrm((DEPTH, D, D_FF), D ** -0.5),
        'w_ff2': nrm((DEPTH, D_FF, D), D_FF ** -0.5),
        'rw_mu': jax.random.uniform(next(ks), (NA, 6, D), f32),
        'rw_w_rkv': nrm((NA, 3, D, D), D ** -0.5),
        'rw_w0': jax.random.uniform(next(ks), (NA, D), f32, -6.0, -1.0),
        'rw_w1': nrm((NA, D, D_DECAY_LORA), D ** -0.5),
        'rw_w2': nrm((NA, D_DECAY_LORA, D), D_DECAY_LORA ** -0.5),
        'rw_a0': nrm((NA, D), 0.1),
        'rw_a1': nrm((NA, D, D_AAA_LORA), D ** -0.5),
        'rw_a2': nrm((NA, D_AAA_LORA, D), D_AAA_LORA ** -0.5),
        'rw_g1': nrm((NA, D, D_GATE_LORA), D ** -0.5),
        'rw_g2': nrm((NA, D_GATE_LORA, D), D_GATE_LORA ** -0.5),
        'rw_k_k': 0.85 + nrm((NA, D), 0.05),
        'rw_k_a': 1.0 + nrm((NA, D), 0.05),
        'rw_r_k': nrm((NA, RWKV_HEADS, RWKV_HEAD), 0.1),
        'rw_gn_w': gain((NA, D)),
        'rw_gn_b': nrm((NA, D), 0.02),
        'rw_w_o': nrm((NA, D, D), D ** -0.5),
        'rw_v0': nrm((NV, D), 0.1),
        'rw_v1': nrm((NV, D, D_MV_LORA), D ** -0.5),
        'rw_v2': nrm((NV, D_MV_LORA, D), D_MV_LORA ** -0.5),
        'kv_norm_g': gain((D,)),
        'w_kv': nrm((D, 2 * KV_HEADS * ATT_HEAD), D ** -0.5),
        'mb_w_q': nrm((NBL, D, Q_HEADS * ATT_HEAD), D ** -0.5),
        'mb_w_o': nrm((NBL, Q_HEADS * ATT_HEAD, D), (Q_HEADS * ATT_HEAD) ** -0.5),
        'final_g': gain((D,)),
    }


def reference(x, ln_mix_g, ln_ffn_g, w_ff1, w_ff2,
              rw_mu, rw_w_rkv, rw_w0, rw_w1, rw_w2, rw_a0, rw_a1, rw_a2,
              rw_g1, rw_g2, rw_k_k, rw_k_a, rw_r_k, rw_gn_w, rw_gn_b, rw_w_o,
              rw_v0, rw_v1, rw_v2, kv_norm_g, w_kv, mb_w_q, mb_w_o, final_g):
    h = x
    v_first = None
    kv = None
    for layer in range(DEPTH):
        hn = rms_norm(h, ln_mix_g[layer])
        if layer < N_A_LAYERS:
            i = layer
            v_lora = None if i == 0 else (rw_v0[i - 1], rw_v1[i - 1], rw_v2[i - 1])
            mix, v_first = rwkv7_time_mix(
                hn, v_first, rw_mu[i], rw_w_rkv[i], rw_w0[i], rw_w1[i], rw_w2[i],
                rw_a0[i], rw_a1[i], rw_a2[i], rw_g1[i], rw_g2[i], rw_k_k[i], rw_k_a[i],
                rw_r_k[i], rw_gn_w[i], rw_gn_b[i], rw_w_o[i], v_lora)
        else:
            if kv is None:
                kv = shared_kv(h, kv_norm_g, w_kv)
            j = layer - N_A_LAYERS
            mix = moba_layer(hn, kv[0], kv[1], kv[2], mb_w_q[j], mb_w_o[j])
        h = h + mix
        h = h + sq_relu_mlp(rms_norm(h, ln_ffn_g[layer]), w_ff1[layer], w_ff2[layer])
    return rms_norm(h, final_g)
```

```python
import functools
import math

import jax
import jax.numpy as jnp
from jax import lax
from jax.experimental import pallas as pl
from jax.experimental.pallas import tpu as pltpu

F32 = jnp.float32
BF16 = jnp.bfloat16

RMS_EPS = 1e-6
GN_EPS = 64e-5
RWKV_HEAD = 64
ATT_HEAD = 128
KV_HEADS = 4
MOBA_BLOCK = 256
MOBA_TOPK = 3
ROPE_THETA = 10000.0
LANES = 128
WKV_CHUNK = 64
MASKED_LOGIT = -1e30
VMEM_LIMIT = 56 * 1024 * 1024


def _cparams(sem):
    return pltpu.CompilerParams(dimension_semantics=sem, vmem_limit_bytes=VMEM_LIMIT)


def _rms(x, g):
    return x * lax.rsqrt(jnp.mean(x * x, axis=-1, keepdims=True) + RMS_EPS) * g


def _dot(a, b):
    return jnp.dot(a.astype(BF16), b.astype(BF16), preferred_element_type=F32)


def _dot_nt(a, b):
    return lax.dot_general(a.astype(BF16), b.astype(BF16), (((1,), (1,)), ((), ())),
                           preferred_element_type=F32)


def _dot_tn(a, b):
    return lax.dot_general(a.astype(BF16), b.astype(BF16), (((0,), (0,)), ((), ())),
                           preferred_element_type=F32)


def _split3(x):
    h1 = x.astype(BF16)
    r1 = x - h1.astype(F32)
    h2 = r1.astype(BF16)
    h3 = (r1 - h2.astype(F32)).astype(BF16)
    return h1, h2, h3


def _rwkv_prep_kernel(*refs, tiles_per_seq, has_v):
    (h_ref, hprev_ref, g_ref, mu_ref, w0_ref, w1_ref, w2_ref, a0_ref, a1_ref, a2_ref,
     g1_ref, g2_ref) = refs[:12]
    if has_v:
        v0_ref, v1_ref, v2_ref = refs[12:15]
        x3_ref, lw_ref, a_ref, gate_ref, vg_ref = refs[15:]
    else:
        x3_ref, lw_ref, a_ref, gate_ref = refs[12:]
    i = pl.program_id(0)
    g = g_ref[...]
    hn = _rms(h_ref[...], g)
    prev = _rms(hprev_ref[7:8, :], g)
    prev = jnp.where(i % tiles_per_seq == 0, 0.0, prev)
    rows = lax.broadcasted_iota(jnp.int32, hn.shape, 0)
    shifted = jnp.where(rows == 0, prev, pltpu.roll(hn, 1, axis=0))
    xx = shifted - hn
    x3_ref[0] = (hn + xx * mu_ref[0:1, :]).astype(BF16)
    x3_ref[1] = (hn + xx * mu_ref[2:3, :]).astype(BF16)
    xv = (hn + xx * mu_ref[3:4, :]).astype(BF16)
    x3_ref[2] = xv
    xw = hn + xx * mu_ref[1:2, :]
    w = w0_ref[...] + _dot(jnp.tanh(_dot(xw, w1_ref[...])), w2_ref[...])
    w = -jax.nn.softplus(-w) - 0.5
    lw_ref[...] = -jnp.exp(w)
    xa = hn + xx * mu_ref[4:5, :]
    a_ref[...] = jax.nn.sigmoid(a0_ref[...] + _dot(_dot(xa, a1_ref[...]), a2_ref[...]))
    xg = hn + xx * mu_ref[5:6, :]
    gate_ref[...] = _dot(jax.nn.sigmoid(_dot(xg, g1_ref[...])), g2_ref[...])
    if has_v:
        vg_ref[...] = jax.nn.sigmoid(v0_ref[...] + _dot(_dot(xv, v1_ref[...]), v2_ref[...]))


def _rwkv_prep(h, seq_len, g, mu, w0, w1, w2, a0, a1, a2, g1, g2, v_lora, tm=256):
    n, d = h.shape
    has_v = v_lora is not None
    row = lambda v: v.reshape(1, d)
    full = lambda arr: pl.BlockSpec(arr.shape, lambda i: (0,) * arr.ndim)
    tile = pl.BlockSpec((tm, d), lambda i: (i, 0))
    ins = [h, h, row(g), mu, row(w0), w1, w2, row(a0), a1, a2, g1, g2]
    in_specs = [tile, pl.BlockSpec((8, d), lambda i: (jnp.maximum(i * (tm // 8) - 1, 0), 0))]
    in_specs += [full(x) for x in ins[2:]]
    out_shape = [jax.ShapeDtypeStruct((3, n, d), BF16)] + [jax.ShapeDtypeStruct((n, d), F32)] * 3
    out_specs = [pl.BlockSpec((3, tm, d), lambda i: (0, i, 0)), tile, tile, tile]
    if has_v:
        v0, v1, v2 = v_lora
        extra = [row(v0), v1, v2]
        ins += extra
        in_specs += [full(x) for x in extra]
        out_shape.append(jax.ShapeDtypeStruct((n, d), F32))
        out_specs.append(tile)
    return pl.pallas_call(
        functools.partial(_rwkv_prep_kernel, tiles_per_seq=seq_len // tm, has_v=has_v),
        grid=(n // tm,), in_specs=in_specs, out_specs=out_specs, out_shape=out_shape,
        compiler_params=_cparams(("parallel",)))(*ins)


def _mm3_kernel(x_ref, w_ref, o_ref):
    o_ref[...] = jnp.dot(x_ref[...], w_ref[...], preferred_element_type=F32)


def _mm3(x3, w3, tm=1024, tn=512):
    p, n, k = x3.shape
    m = w3.shape[2]
    tm, tn = min(tm, n), min(tn, m)
    return pl.pallas_call(
        _mm3_kernel, grid=(p, n // tm, m // tn),
        in_specs=[pl.BlockSpec((None, tm, k), lambda q, i, j: (q, i, 0)),
                  pl.BlockSpec((None, k, tn), lambda q, i, j: (q, 0, j))],
        out_specs=pl.BlockSpec((None, tm, tn), lambda q, i, j: (q, i, j)),
        out_shape=jax.ShapeDtypeStruct((p, n, m), F32),
        compiler_params=_cparams(("parallel", "parallel", "arbitrary")))(x3, w3)


def _mm_res_kernel(x_ref, w_ref, r_ref, o_ref):
    o_ref[...] = r_ref[...] + jnp.dot(x_ref[...], w_ref[...], preferred_element_type=F32)


def _mm_res(x, w, res, tm=1024, tn=512):
    n, k = x.shape
    m = w.shape[1]
    tm, tn = min(tm, n), min(tn, m)
    return pl.pallas_call(
        _mm_res_kernel, grid=(n // tm, m // tn),
        in_specs=[pl.BlockSpec((tm, k), lambda i, j: (i, 0)),
                  pl.BlockSpec((k, tn), lambda i, j: (0, j)),
                  pl.BlockSpec((tm, tn), lambda i, j: (i, j))],
        out_specs=pl.BlockSpec((tm, tn), lambda i, j: (i, j)),
        out_shape=jax.ShapeDtypeStruct((n, m), F32),
        compiler_params=_cparams(("parallel", "arbitrary")))(x, w, res)


def _norm_mm_rope_kernel(h_ref, g_ref, w_ref, cos_ref, sin_ref, o_ref, hn_ref, *, rope_tiles):
    j = pl.program_id(1)

    @pl.when(j == 0)
    def _():
        hn_ref[...] = _rms(h_ref[...], g_ref[...]).astype(BF16)

    acc = jnp.dot(hn_ref[...], w_ref[...], preferred_element_type=F32)

    @pl.when(j < rope_tiles)
    def _():
        cos, sin = cos_ref[...], sin_ref[...]
        for hh in range(acc.shape[1] // ATT_HEAD):
            x = acc[:, hh * ATT_HEAD:(hh + 1) * ATT_HEAD]
            o_ref[:, hh * ATT_HEAD:(hh + 1) * ATT_HEAD] = (
                x * cos + pltpu.roll(x, ATT_HEAD // 2, axis=1) * sin)

    @pl.when(j >= rope_tiles)
    def _():
        o_ref[...] = acc


def _norm_mm_rope(h, g, w, cos, sin, rope_cols, seq_len, tm=512, tn=512):
    n, d = h.shape
    m = w.shape[1]
    return pl.pallas_call(
        functools.partial(_norm_mm_rope_kernel, rope_tiles=rope_cols // tn),
        grid=(n // tm, m // tn),
        in_specs=[pl.BlockSpec((tm, d), lambda i, j: (i, 0)),
                  pl.BlockSpec((1, d), lambda i, j: (0, 0)),
                  pl.BlockSpec((d, tn), lambda i, j: (0, j)),
                  pl.BlockSpec((tm, ATT_HEAD), lambda i, j: (i % (seq_len // tm), 0)),
                  pl.BlockSpec((tm, ATT_HEAD), lambda i, j: (i % (seq_len // tm), 0))],
        out_specs=pl.BlockSpec((tm, tn), lambda i, j: (i, j)),
        out_shape=jax.ShapeDtypeStruct((n, m), F32),
        scratch_shapes=[pltpu.VMEM((tm, d), BF16)],
        compiler_params=_cparams(("parallel", "arbitrary")))(h, g.reshape(1, d), w, cos, sin)


def _mlp_kernel(h_ref, g_ref, w1_ref, w2_ref, gf_ref, o_ref, hn_ref, acc_ref, *, final_norm):
    j = pl.program_id(1)

    @pl.when(j == 0)
    def _():
        hn_ref[...] = _rms(h_ref[...], g_ref[...]).astype(BF16)
        acc_ref[...] = jnp.zeros_like(acc_ref)

    u = jnp.dot(hn_ref[...], w1_ref[...], preferred_element_type=F32)
    u = jnp.square(jnp.maximum(u, 0.0)).astype(BF16)
    acc_ref[...] += jnp.dot(u, w2_ref[...], preferred_element_type=F32)

    @pl.when(j == pl.num_programs(1) - 1)
    def _():
        out = h_ref[...] + acc_ref[...]
        if final_norm:
            out = _rms(out, gf_ref[...])
        o_ref[...] = out


def _mlp(h, g, w1, w2, gf, final_norm, tm=512, tf=512):
    n, d = h.shape
    f = w1.shape[1]
    return pl.pallas_call(
        functools.partial(_mlp_kernel, final_norm=final_norm),
        grid=(n // tm, f // tf),
        in_specs=[pl.BlockSpec((tm, d), lambda i, j: (i, 0)),
                  pl.BlockSpec((1, d), lambda i, j: (0, 0)),
                  pl.BlockSpec((d, tf), lambda i, j: (0, j)),
                  pl.BlockSpec((tf, d), lambda i, j: (j, 0)),
                  pl.BlockSpec((1, d), lambda i, j: (0, 0))],
        out_specs=pl.BlockSpec((tm, d), lambda i, j: (i, 0)),
        out_shape=jax.ShapeDtypeStruct((n, d), F32),
        scratch_shapes=[pltpu.VMEM((tm, d), BF16), pltpu.VMEM((tm, d), F32)],
        compiler_params=_cparams(("parallel", "arbitrary")))(
            h, g.reshape(1, d), w1, w2, gf.reshape(1, d))


def _wkv_pair(r, k, v, lw, a, gate, kkp, kap, rkp, gnw, gnb, state):
    L = r.shape[0]
    half = RWKV_HEAD
    lane = lax.broadcasted_iota(jnp.int32, r.shape, 1)
    m0 = lane < half

    def seg_sum(x):
        s0 = jnp.sum(jnp.where(m0, x, 0.0), axis=-1, keepdims=True)
        s1 = jnp.sum(jnp.where(m0, 0.0, x), axis=-1, keepdims=True)
        return jnp.where(m0, s0, s1)

    def stack(x):
        return jnp.concatenate([jnp.where(m0, x, 0.0), jnp.where(m0, 0.0, x)], axis=0)

    kq = k * kkp
    kk = kq / jnp.maximum(jnp.sqrt(seg_sum(kq * kq)), 1e-12)
    k2 = k * (1.0 + (a - 1.0) * kap)
    b = kk * a

    ti = lax.broadcasted_iota(jnp.int32, (L, L), 0)
    tj = lax.broadcasted_iota(jnp.int32, (L, L), 1)
    tri = (tj <= ti).astype(BF16)
    l1, l2, l3 = _split3(lw)
    c = (jnp.dot(tri, l1, preferred_element_type=F32) + jnp.dot(tri, l2, preferred_element_type=F32)
         + jnp.dot(tri, l3, preferred_element_type=F32))
    c_last = c[L - 1:L, :]
    e_neg = jnp.exp(-c)
    e_last = jnp.exp(c_last - c)
    at = stack(-kk * jnp.exp(c - lw))
    rt = stack(r * jnp.exp(c))
    bt = stack(b * e_neg)
    kt = stack(k2 * e_neg)
    vs = stack(v)

    n2 = 2 * L
    afull = _dot_nt(jnp.concatenate([at, rt], axis=0), jnp.concatenate([bt, kt], axis=0))
    row = lax.broadcasted_iota(jnp.int32, afull.shape, 0)
    col = lax.broadcasted_iota(jnp.int32, afull.shape, 1)
    tr, tc = row % L, col % L
    afull = jnp.where(tc < tr + jnp.where(row < n2, 0, 1), afull, 0.0)
    a_ab = afull[:n2, :n2]
    a_ak = afull[:n2, n2:]
    a_bot = afull[n2:, :]

    x = jnp.concatenate([at, _dot(a_ak, vs)], axis=1)
    p = a_ab
    steps = int(math.log2(L))
    for s in range(steps):
        x = x + _dot(p, x)
        if s + 1 < steps:
            p = _dot(p, p)
    w_st, u0 = x[:, :LANES], x[:, LANES:]

    wh = _dot_nt(jnp.concatenate([w_st, rt], axis=0), state)
    u = wh[:n2] + u0
    uv = jnp.concatenate([u, vs], axis=0)
    y_st = wh[n2:] + _dot(a_bot, uv)
    y = y_st[:L] + y_st[L:]
    new_state = state * jnp.exp(c_last) + _dot_tn(
        uv, jnp.concatenate([stack(b * e_last), stack(k2 * e_last)], axis=0))

    mean = seg_sum(y) * (1.0 / half)
    dlt = y - mean
    var = seg_sum(dlt * dlt) * (1.0 / half)
    yn = dlt * lax.rsqrt(var + GN_EPS) * gnw + gnb
    bonus = seg_sum(r * k2 * rkp) * v
    return (yn + bonus) * gate, new_state


def _wkv_kernel(*refs, has_v):
    r_ref, k_ref, v_ref, lw_ref, a_ref, gate_ref = refs[:6]
    nxt = 6
    if has_v:
        vf_ref, vg_ref = refs[6:8]
        nxt = 8
    kk_ref, ka_ref, rk_ref, gnw_ref, gnb_ref, o_ref, st_ref = refs[nxt:]

    @pl.when(pl.program_id(2) == 0)
    def _():
        st_ref[...] = jnp.zeros_like(st_ref)

    for p in range(o_ref.shape[1] // LANES):
        sl = slice(p * LANES, (p + 1) * LANES)
        v = v_ref[:, sl]
        if has_v:
            v = v + (vf_ref[:, sl] - v) * vg_ref[:, sl]
        out, new_state = _wkv_pair(
            r_ref[:, sl], k_ref[:, sl], v, lw_ref[:, sl], a_ref[:, sl], gate_ref[:, sl],
            kk_ref[:, sl], ka_ref[:, sl], rk_ref[:, sl], gnw_ref[:, sl], gnb_ref[:, sl], st_ref[p])
        st_ref[p] = new_state
        o_ref[:, sl] = out.astype(BF16)


def _wkv(rkv, lw, a, gate, v_first, v_gate, k_k, k_a, r_k, gn_w, gn_b, batch, width=512):
    _, n, d = rkv.shape
    width = min(width, d)
    L = WKV_CHUNK
    nc = n // batch // L
    has_v = v_first is not None
    tile = pl.BlockSpec((L, width), lambda b, p, c: (b * nc + c, p))
    rkv_spec = lambda q: pl.BlockSpec((None, L, width), lambda b, p, c: (q, b * nc + c, p))
    prm = pl.BlockSpec((1, width), lambda b, p, c: (0, p))
    ins = [rkv, rkv, rkv, lw, a, gate]
    in_specs = [rkv_spec(0), rkv_spec(1), rkv_spec(2), tile, tile, tile]
    if has_v:
        ins += [v_first, v_gate]
        in_specs += [rkv_spec(2), tile]
    ins += [x.reshape(1, d) for x in (k_k, k_a, r_k, gn_w, gn_b)]
    in_specs += [prm] * 5
    return pl.pallas_call(
        functools.partial(_wkv_kernel, has_v=has_v),
        grid=(batch, d // width, nc), in_specs=in_specs, out_specs=tile,
        out_shape=jax.ShapeDtypeStruct((n, d), BF16),
        scratch_shapes=[pltpu.VMEM((width // LANES, LANES, LANES), F32)],
        compiler_params=_cparams(("parallel", "parallel", "arbitrary")))(*ins)


def _moba_kernel(q_ref, k_ref, v_ref, o_ref):
    bs = MOBA_BLOCK
    t = k_ref.shape[0]
    nb = t // bs
    scale = ATT_HEAD ** -0.5
    kf = k_ref[...]
    kb = kf.astype(BF16)
    vb = v_ref[...].astype(BF16)
    kmean = jnp.mean(kf.reshape(nb, bs, ATT_HEAD), axis=1)
    kmean = jnp.concatenate([kmean, jnp.zeros((LANES - nb, ATT_HEAD), F32)], axis=0)
    km1, km2, _ = _split3(kmean)
    lane = lax.broadcasted_iota(jnp.int32, (bs, LANES), 1)
    qi = lax.broadcasted_iota(jnp.int32, (bs, bs), 0)
    ki = lax.broadcasted_iota(jnp.int32, (bs, bs), 1)
    causal = ki <= qi
    for qb in range(nb):
        q = q_ref[qb * bs:(qb + 1) * bs, :]
        nk = (qb + 1) * bs
        logits = _dot_nt(q, kb[:nk]) * scale
        pieces = [logits[:, :qb * bs]] if qb > 0 else []
        if qb > MOBA_TOPK:
            q1, q2, _ = _split3(q)
            gate = _dot_nt(q1, km1) + _dot_nt(q2, km1) + _dot_nt(q1, km2)
            cnt = jnp.zeros((bs, LANES), jnp.int32)
            for m in range(qb):
                gm = gate[:, m:m + 1]
                beats = (gm > gate) | ((gm == gate) & (m < lane))
                cnt = cnt + beats.astype(jnp.int32)
            bias = jnp.where(cnt < MOBA_TOPK, 0.0, MASKED_LOGIT)
            pieces = [logits[:, n * bs:(n + 1) * bs] + bias[:, n:n + 1] for n in range(qb)]
        pieces.append(jnp.where(causal, logits[:, qb * bs:], MASKED_LOGIT))
        logits = jnp.concatenate(pieces, axis=1) if qb > 0 else pieces[0]
        e = jnp.exp(logits - jnp.max(logits, axis=-1, keepdims=True))
        s = jnp.sum(e, axis=-1, keepdims=True)
        o_ref[qb * bs:(qb + 1) * bs, :] = (_dot(e, vb[:nk]) / s).astype(BF16)


def _moba(q, kv, batch):
    n, dq = q.shape
    t = n // batch
    group = dq // ATT_HEAD // KV_HEADS
    return pl.pallas_call(
        _moba_kernel, grid=(batch, KV_HEADS, group),
        in_specs=[pl.BlockSpec((t, ATT_HEAD), lambda b, kh, g: (b, kh * group + g)),
                  pl.BlockSpec((t, ATT_HEAD), lambda b, kh, g: (b, kh)),
                  pl.BlockSpec((t, ATT_HEAD), lambda b, kh, g: (b, KV_HEADS + kh))],
        out_specs=pl.BlockSpec((t, ATT_HEAD), lambda b, kh, g: (b, kh * group + g)),
        out_shape=jax.ShapeDtypeStruct((n, dq), BF16),
        compiler_params=_cparams(("parallel", "parallel", "arbitrary")))(q, kv, kv)


def _pad_lora(w_in, w_out):
    r = w_in.shape[1]
    rp = -(-r // LANES) * LANES
    return (jnp.pad(w_in, ((0, 0), (0, rp - r))).astype(BF16),
            jnp.pad(w_out, ((0, rp - r), (0, 0))).astype(BF16))


def _rope_tables(t):
    half = ATT_HEAD // 2
    inv = ROPE_THETA ** (-jnp.arange(half, dtype=F32) / half)
    ang = jnp.arange(t, dtype=F32)[:, None] * inv[None, :]
    cos, sin = jnp.cos(ang), jnp.sin(ang)
    return jnp.concatenate([cos, cos], axis=-1), jnp.concatenate([-sin, sin], axis=-1)


def kernel(x, ln_mix_g, ln_ffn_g, w_ff1, w_ff2, rw_mu, rw_w_rkv, rw_w0, rw_w1, rw_w2, rw_a0, rw_a1, rw_a2, rw_g1, rw_g2, rw_k_k, rw_k_a, rw_r_k, rw_gn_w, rw_gn_b, rw_w_o, rw_v0, rw_v1, rw_v2, kv_norm_g, w_kv, mb_w_q, mb_w_o, final_g):
    batch, t, d = x.shape
    n = batch * t
    depth = ln_mix_g.shape[0]
    n_rwkv = rw_mu.shape[0]
    h = x.reshape(n, d)
    cos, sin = _rope_tables(t)
    v_first = None
    kv = None
    for layer in range(depth):
        if layer < n_rwkv:
            i = layer
            w1, w2 = _pad_lora(rw_w1[i], rw_w2[i])
            a1, a2 = _pad_lora(rw_a1[i], rw_a2[i])
            g1, g2 = _pad_lora(rw_g1[i], rw_g2[i])
            v_lora = None
            if i > 0:
                v1, v2 = _pad_lora(rw_v1[i - 1], rw_v2[i - 1])
                v_lora = (rw_v0[i - 1], v1, v2)
            prep = _rwkv_prep(h, t, ln_mix_g[layer], rw_mu[i], rw_w0[i], w1, w2,
                              rw_a0[i], a1, a2, g1, g2, v_lora)
            x3, lw, a, gate = prep[:4]
            v_gate = prep[4] if i > 0 else None
            rkv = _mm3(x3, rw_w_rkv[i].astype(BF16))
            yg = _wkv(rkv, lw, a, gate, v_first, v_gate, rw_k_k[i], rw_k_a[i], rw_r_k[i],
                      rw_gn_w[i], rw_gn_b[i], batch)
            if i == 0:
                v_first = rkv
            h = _mm_res(yg, rw_w_o[i].astype(BF16), h)
        else:
            j = layer - n_rwkv
            if kv is None:
                kv = _norm_mm_rope(h, kv_norm_g, w_kv.astype(BF16), cos, sin,
                                   KV_HEADS * ATT_HEAD, t)
            q = _norm_mm_rope(h, ln_mix_g[layer], mb_w_q[j].astype(BF16), cos, sin,
                              mb_w_q.shape[2], t)
            att = _moba(q, kv, batch)
            h = _mm_res(att, mb_w_o[j].astype(BF16), h)
        h = _mlp(h, ln_ffn_g[layer], w_ff1[layer].astype(BF16), w_ff2[layer].astype(BF16),
                 final_g, layer == depth - 1)
    return h.reshape(batch, t, d)
```

```python
import functools
import math

import jax
import jax.numpy as jnp
from jax import lax
from jax.experimental import pallas as pl
from jax.experimental.pallas import tpu as pltpu

F32 = jnp.float32
BF16 = jnp.bfloat16

RMS_EPS = 1e-6
GN_EPS = 64e-5
RWKV_HEAD = 64
ATT_HEAD = 128
KV_HEADS = 4
MOBA_BLOCK = 256
MOBA_TOPK = 3
ROPE_THETA = 10000.0
LANES = 128
WKV_CHUNK = 64
MASKED_LOGIT = -1e30
VMEM_LIMIT = 56 * 1024 * 1024


def _cparams(sem):
    return pltpu.CompilerParams(dimension_semantics=sem, vmem_limit_bytes=VMEM_LIMIT)


def _rms(x, g):
    return x * lax.rsqrt(jnp.mean(x * x, axis=-1, keepdims=True) + RMS_EPS) * g


def _dot(a, b):
    return jnp.dot(a.astype(BF16), b.astype(BF16), preferred_element_type=F32)


def _dot_nt(a, b):
    return lax.dot_general(a.astype(BF16), b.astype(BF16), (((1,), (1,)), ((), ())),
                           preferred_element_type=F32)


def _dot_tn(a, b):
    return lax.dot_general(a.astype(BF16), b.astype(BF16), (((0,), (0,)), ((), ())),
                           preferred_element_type=F32)


def _split3(x):
    h1 = x.astype(BF16)
    r1 = x - h1.astype(F32)
    h2 = r1.astype(BF16)
    h3 = (r1 - h2.astype(F32)).astype(BF16)
    return h1, h2, h3


def _rwkv_prep_kernel(*refs, tiles_per_seq, has_v):
    (h_ref, hprev_ref, g_ref, mu_ref, w0_ref, w1_ref, w2_ref, a0_ref, a1_ref, a2_ref,
     g1_ref, g2_ref) = refs[:12]
    if has_v:
        v0_ref, v1_ref, v2_ref = refs[12:15]
        x3_ref, lw_ref, a_ref, gate_ref, vg_ref = refs[15:]
    else:
        x3_ref, lw_ref, a_ref, gate_ref = refs[12:]
    i = pl.program_id(0)
    g = g_ref[...]
    hn = _rms(h_ref[...], g)
    prev = _rms(hprev_ref[7:8, :], g)
    prev = jnp.where(i % tiles_per_seq == 0, 0.0, prev)
    rows = lax.broadcasted_iota(jnp.int32, hn.shape, 0)
    shifted = jnp.where(rows == 0, prev, pltpu.roll(hn, 1, axis=0))
    xx = shifted - hn
    x3_ref[0] = (hn + xx * mu_ref[0:1, :]).astype(BF16)
    x3_ref[1] = (hn + xx * mu_ref[2:3, :]).astype(BF16)
    xv = (hn + xx * mu_ref[3:4, :]).astype(BF16)
    x3_ref[2] = xv
    xw = hn + xx * mu_ref[1:2, :]
    w = w0_ref[...] + _dot(jnp.tanh(_dot(xw, w1_ref[...])), w2_ref[...])
    w = -jax.nn.softplus(-w) - 0.5
    lw_ref[...] = -jnp.exp(w)
    xa = hn + xx * mu_ref[4:5, :]
    a_ref[...] = jax.nn.sigmoid(a0_ref[...] + _dot(_dot(xa, a1_ref[...]), a2_ref[...]))
    xg = hn + xx * mu_ref[5:6, :]
    gate_ref[...] = _dot(jax.nn.sigmoid(_dot(xg, g1_ref[...])), g2_ref[...])
    if has_v:
        vg_ref[...] = jax.nn.sigmoid(v0_ref[...] + _dot(_dot(xv, v1_ref[...]), v2_ref[...]))


def _rwkv_prep(h, seq_len, g, mu, w0, w1, w2, a0, a1, a2, g1, g2, v_lora, tm=256):
    n, d = h.shape
    has_v = v_lora is not None
    row = lambda v: v.reshape(1, d)
    full = lambda arr: pl.BlockSpec(arr.shape, lambda i: (0,) * arr.ndim)
    tile = pl.BlockSpec((tm, d), lambda i: (i, 0))
    ins = [h, h, row(g), mu, row(w0), w1, w2, row(a0), a1, a2, g1, g2]
    in_specs = [tile, pl.BlockSpec((8, d), lambda i: (jnp.maximum(i * (tm // 8) - 1, 0), 0))]
    in_specs += [full(x) for x in ins[2:]]
    out_shape = [jax.ShapeDtypeStruct((3, n, d), BF16)] + [jax.ShapeDtypeStruct((n, d), F32)] * 3
    out_specs = [pl.BlockSpec((3, tm, d), lambda i: (0, i, 0)), tile, tile, tile]
    if has_v:
        v0, v1, v2 = v_lora
        extra = [row(v0), v1, v2]
        ins += extra
        in_specs += [full(x) for x in extra]
        out_shape.append(jax.ShapeDtypeStruct((n, d), F32))
        out_specs.append(tile)
    return pl.pallas_call(
        functools.partial(_rwkv_prep_kernel, tiles_per_seq=seq_len // tm, has_v=has_v),
        grid=(n // tm,), in_specs=in_specs, out_specs=out_specs, out_shape=out_shape,
        compiler_params=_cparams(("parallel",)))(*ins)


def _mm3_kernel(x_ref, w_ref, o_ref):
    o_ref[...] = jnp.dot(x_ref[...], w_ref[...], preferred_element_type=F32)


def _mm3(x3, w3, tm=1024, tn=512):
    p, n, k = x3.shape
    m = w3.shape[2]
    tm, tn = min(tm, n), min(tn, m)
    return pl.pallas_call(
        _mm3_kernel, grid=(p, n // tm, m // tn),
        in_specs=[pl.BlockSpec((None, tm, k), lambda q, i, j: (q, i, 0)),
                  pl.BlockSpec((None, k, tn), lambda q, i, j: (q, 0, j))],
        out_specs=pl.BlockSpec((None, tm, tn), lambda q, i, j: (q, i, j)),
        out_shape=jax.ShapeDtypeStruct((p, n, m), F32),
        compiler_params=_cparams(("parallel", "parallel", "arbitrary")))(x3, w3)


def _mm_res_kernel(x_ref, w_ref, r_ref, o_ref):
    o_ref[...] = r_ref[...] + jnp.dot(x_ref[...], w_ref[...], preferred_element_type=F32)


def _mm_res(x, w, res, tm=1024, tn=512):
    n, k = x.shape
    m = w.shape[1]
    tm, tn = min(tm, n), min(tn, m)
    return pl.pallas_call(
        _mm_res_kernel, grid=(n // tm, m // tn),
        in_specs=[pl.BlockSpec((tm, k), lambda i, j: (i, 0)),
                  pl.BlockSpec((k, tn), lambda i, j: (0, j)),
                  pl.BlockSpec((tm, tn), lambda i, j: (i, j))],
        out_specs=pl.BlockSpec((tm, tn), lambda i, j: (i, j)),
        out_shape=jax.ShapeDtypeStruct((n, m), F32),
        compiler_params=_cparams(("parallel", "arbitrary")))(x, w, res)


def _norm_mm_rope_kernel(h_ref, g_ref, w_ref, cos_ref, sin_ref, o_ref, hn_ref, *, rope_tiles):
    j = pl.program_id(1)

    @pl.when(j == 0)
    def _():
        hn_ref[...] = _rms(h_ref[...], g_ref[...]).astype(BF16)

    acc = jnp.dot(hn_ref[...], w_ref[...], preferred_element_type=F32)

    @pl.when(j < rope_tiles)
    def _():
        cos, sin = cos_ref[...], sin_ref[...]
        for hh in range(acc.shape[1] // ATT_HEAD):
            x = acc[:, hh * ATT_HEAD:(hh + 1) * ATT_HEAD]
            o_ref[:, hh * ATT_HEAD:(hh + 1) * ATT_HEAD] = (
                x * cos + pltpu.roll(x, ATT_HEAD // 2, axis=1) * sin)

    @pl.when(j >= rope_tiles)
    def _():
        o_ref[...] = acc


def _norm_mm_rope(h, g, w, cos, sin, rope_cols, seq_len, tm=512, tn=512):
    n, d = h.shape
    m = w.shape[1]
    return pl.pallas_call(
        functools.partial(_norm_mm_rope_kernel, rope_tiles=rope_cols // tn),
        grid=(n // tm, m // tn),
        in_specs=[pl.BlockSpec((tm, d), lambda i, j: (i, 0)),
                  pl.BlockSpec((1, d), lambda i, j: (0, 0)),
                  pl.BlockSpec((d, tn), lambda i, j: (0, j)),
                  pl.BlockSpec((tm, ATT_HEAD), lambda i, j: (i % (seq_len // tm), 0)),
                  pl.BlockSpec((tm, ATT_HEAD), lambda i, j: (i % (seq_len // tm), 0))],
        out_specs=pl.BlockSpec((tm, tn), lambda i, j: (i, j)),
        out_shape=jax.ShapeDtypeStruct((n, m), F32),
        scratch_shapes=[pltpu.VMEM((tm, d), BF16)],
        compiler_params=_cparams(("parallel", "arbitrary")))(h, g.reshape(1, d), w, cos, sin)


def _mlp_kernel(h_ref, g_ref, w1_ref, w2_ref, gf_ref, o_ref, hn_ref, *, final_norm):
    j = pl.program_id(1)

    @pl.when(j == 0)
    def _():
        h = h_ref[...]
        hn_ref[...] = _rms(h, g_ref[...]).astype(BF16)
        o_ref[...] = h

    u = jnp.dot(hn_ref[...], w1_ref[...], preferred_element_type=F32)
    u = jnp.square(jnp.maximum(u, 0.0)).astype(BF16)
    o_ref[...] += jnp.dot(u, w2_ref[...], preferred_element_type=F32)

    if final_norm:
        @pl.when(j == pl.num_programs(1) - 1)
        def _():
            o_ref[...] = _rms(o_ref[...], gf_ref[...])


def _mlp(h, g, w1, w2, gf, final_norm, tm=1024, tf=512):
    n, d = h.shape
    f = w1.shape[1]
    return pl.pallas_call(
        functools.partial(_mlp_kernel, final_norm=final_norm),
        grid=(n // tm, f // tf),
        in_specs=[pl.BlockSpec((tm, d), lambda i, j: (i, 0)),
                  pl.BlockSpec((1, d), lambda i, j: (0, 0)),
                  pl.BlockSpec((d, tf), lambda i, j: (0, j)),
                  pl.BlockSpec((tf, d), lambda i, j: (j, 0)),
                  pl.BlockSpec((1, d), lambda i, j: (0, 0))],
        out_specs=pl.BlockSpec((tm, d), lambda i, j: (i, 0)),
        out_shape=jax.ShapeDtypeStruct((n, d), F32),
        scratch_shapes=[pltpu.VMEM((tm, d), BF16)],
        compiler_params=_cparams(("parallel", "arbitrary")))(
            h, g.reshape(1, d), w1, w2, gf.reshape(1, d))


def _wkv_kernel(*refs, has_v):
    r_ref, k_ref, v_ref, lw_ref, a_ref, gate_ref = refs[:6]
    nxt = 6
    if has_v:
        vf_ref, vg_ref = refs[6:8]
        nxt = 8
    kk_ref, ka_ref, rk_ref, gnw_ref, gnb_ref, o_ref, st_ref = refs[nxt:]

    @pl.when(pl.program_id(2) == 0)
    def _():
        st_ref[...] = jnp.zeros_like(st_ref)

    L = o_ref.shape[0]
    n2 = 2 * L
    half = RWKV_HEAD
    pairs = range(o_ref.shape[1] // LANES)
    lane = lax.broadcasted_iota(jnp.int32, (L, LANES), 1)
    m0 = lane < half
    trow = lax.broadcasted_iota(jnp.int32, (L, LANES), 0)

    def seg_sum(x):
        s0 = jnp.sum(jnp.where(m0, x, 0.0), axis=-1, keepdims=True)
        s1 = jnp.sum(jnp.where(m0, 0.0, x), axis=-1, keepdims=True)
        return jnp.where(m0, s0, s1)

    def stack(x):
        return jnp.concatenate([jnp.where(m0, x, 0.0), jnp.where(m0, 0.0, x)], axis=0)

    pre = []
    for p in pairs:
        sl = slice(p * LANES, (p + 1) * LANES)
        r, k, v, lw, a = r_ref[:, sl], k_ref[:, sl], v_ref[:, sl], lw_ref[:, sl], a_ref[:, sl]
        if has_v:
            v = v + (vf_ref[:, sl] - v) * vg_ref[:, sl]
        kq = k * kk_ref[:, sl]
        kk = kq / jnp.maximum(jnp.sqrt(seg_sum(kq * kq)), 1e-12)
        k2 = k * (1.0 + (a - 1.0) * ka_ref[:, sl])
        b = kk * a
        c = lw
        step = 1
        while step < L:
            c = c + jnp.where(trow >= step, pltpu.roll(c, step, axis=0), 0.0)
            step *= 2
        c_last = c[L - 1:L, :]
        e_neg = jnp.exp(-c)
        e_last = jnp.exp(c_last - c)
        pre.append(dict(
            r=r, v=v, k2=k2, e_last_row=jnp.exp(c_last),
            at=stack(-kk * jnp.exp(c - lw)), rt=stack(r * jnp.exp(c)),
            bt=stack(b * e_neg), kt=stack(k2 * e_neg), vs=stack(v),
            bk=jnp.concatenate([stack(b * e_last), stack(k2 * e_last)], axis=0)))

    row = lax.broadcasted_iota(jnp.int32, (2 * n2, 2 * n2), 0)
    col = lax.broadcasted_iota(jnp.int32, (2 * n2, 2 * n2), 1)
    keep = (col % L) < (row % L) + jnp.where(row < n2, 0, 1)
    afull = [jnp.where(keep, _dot_nt(jnp.concatenate([d["at"], d["rt"]], axis=0),
                                     jnp.concatenate([d["bt"], d["kt"]], axis=0)), 0.0)
             for d in pre]
    xs = [jnp.concatenate([d["at"], _dot(af[:n2, n2:], d["vs"])], axis=1)
          for d, af in zip(pre, afull)]
    ps = [af[:n2, :n2] for af in afull]
    steps = int(math.log2(L))
    for s in range(steps):
        xs = [x + _dot(p, x) for p, x in zip(ps, xs)]
        if s + 1 < steps:
            ps = [_dot(p, p) for p in ps]

    whs = [_dot_nt(jnp.concatenate([x[:, :LANES], d["rt"]], axis=0), st_ref[i])
           for i, (d, x) in enumerate(zip(pre, xs))]
    uvs = [jnp.concatenate([wh[:n2] + x[:, LANES:], d["vs"]], axis=0)
           for d, x, wh in zip(pre, xs, whs)]
    ysts = [wh[n2:] + _dot(af[n2:, :], uv) for wh, af, uv in zip(whs, afull, uvs)]
    for i, (d, uv) in enumerate(zip(pre, uvs)):
        st_ref[i] = st_ref[i] * d["e_last_row"] + _dot_tn(uv, d["bk"])

    for p, (d, y_st) in enumerate(zip(pre, ysts)):
        sl = slice(p * LANES, (p + 1) * LANES)
        y = y_st[:L] + y_st[L:]
        mean = seg_sum(y) * (1.0 / half)
        dlt = y - mean
        var = seg_sum(dlt * dlt) * (1.0 / half)
        yn = dlt * lax.rsqrt(var + GN_EPS) * gnw_ref[:, sl] + gnb_ref[:, sl]
        bonus = seg_sum(d["r"] * d["k2"] * rk_ref[:, sl]) * d["v"]
        o_ref[:, sl] = ((yn + bonus) * gate_ref[:, sl]).astype(BF16)


def _wkv(rkv, lw, a, gate, v_first, v_gate, k_k, k_a, r_k, gn_w, gn_b, batch, width=2048):
    _, n, d = rkv.shape
    width = min(width, d)
    L = WKV_CHUNK
    nc = n // batch // L
    has_v = v_first is not None
    tile = pl.BlockSpec((L, width), lambda b, p, c: (b * nc + c, p))
    rkv_spec = lambda q: pl.BlockSpec((None, L, width), lambda b, p, c: (q, b * nc + c, p))
    prm = pl.BlockSpec((1, width), lambda b, p, c: (0, p))
    ins = [rkv, rkv, rkv, lw, a, gate]
    in_specs = [rkv_spec(0), rkv_spec(1), rkv_spec(2), tile, tile, tile]
    if has_v:
        ins += [v_first, v_gate]
        in_specs += [rkv_spec(2), tile]
    ins += [x.reshape(1, d) for x in (k_k, k_a, r_k, gn_w, gn_b)]
    in_specs += [prm] * 5
    return pl.pallas_call(
        functools.partial(_wkv_kernel, has_v=has_v),
        grid=(batch, d // width, nc), in_specs=in_specs, out_specs=tile,
        out_shape=jax.ShapeDtypeStruct((n, d), BF16),
        scratch_shapes=[pltpu.VMEM((width // LANES, LANES, LANES), F32)],
        compiler_params=_cparams(("parallel", "parallel", "arbitrary")))(*ins)


def _moba_kernel(q_ref, k_ref, v_ref, o_ref):
    bs = MOBA_BLOCK
    t = k_ref.shape[0]
    nb = t // bs
    nbp = 16
    qscale = (ATT_HEAD ** -0.5) * math.log2(math.e)
    kf = k_ref[...]
    kmean = jnp.mean(kf.reshape(nb, bs, ATT_HEAD), axis=1)
    kmean = jnp.concatenate([kmean, jnp.zeros((nbp - nb, ATT_HEAD), F32)], axis=0)
    km1, km2, _ = _split3(kmean)
    key_blk = lax.broadcasted_iota(jnp.int32, (t, LANES), 0) // bs
    key_lane = lax.broadcasted_iota(jnp.int32, (t, LANES), 1)
    k_aug = jnp.concatenate(
        [kf.astype(BF16), jnp.where(key_blk == key_lane, MASKED_LOGIT, 0.0).astype(BF16)], axis=1)
    v_aug = jnp.concatenate(
        [v_ref[...].astype(BF16), jnp.where(key_lane == 0, 1.0, 0.0).astype(BF16)], axis=1)
    blk = lax.broadcasted_iota(jnp.int32, (nbp, bs), 0)
    eye = (lax.broadcasted_iota(jnp.int32, (nbp, LANES), 0)
           == lax.broadcasted_iota(jnp.int32, (nbp, LANES), 1)).astype(BF16)
    qi = lax.broadcasted_iota(jnp.int32, (bs, bs), 0)
    ki = lax.broadcasted_iota(jnp.int32, (bs, bs), 1)
    causal_bias = jnp.where(ki <= qi, 0.0, MASKED_LOGIT)
    for qb in range(nb):
        q = q_ref[qb * bs:(qb + 1) * bs, :]
        qs = (q * qscale).astype(BF16)
        own = slice(qb * bs, (qb + 1) * bs)
        l_own = _dot_nt(qs, k_aug[own, :ATT_HEAD]) + causal_bias
        m = jnp.max(l_own, axis=-1, keepdims=True)
        if qb > 0:
            if qb > MOBA_TOPK:
                q1, q2, _ = _split3(q)
                gate = _dot_nt(km1, q1) + _dot_nt(km1, q2) + _dot_nt(km2, q1)
                cnt = jnp.zeros((nbp, bs), jnp.int32)
                for mb in range(qb):
                    gm = gate[mb:mb + 1, :]
                    beats = (gm > gate) | ((gm == gate) & (mb < blk))
                    cnt = cnt + beats.astype(jnp.int32)
                notsel = jnp.where((cnt >= MOBA_TOPK) & (blk < qb), 1.0, 0.0)
                flags = _dot_tn(notsel, eye).astype(BF16)
            else:
                flags = jnp.zeros((bs, LANES), BF16)
            l_past = _dot_nt(jnp.concatenate([qs, flags], axis=1), k_aug[:qb * bs])
            m = jnp.maximum(m, jnp.max(l_past, axis=-1, keepdims=True))
            acc = (_dot(jnp.exp2(l_past - m), v_aug[:qb * bs]) + _dot(jnp.exp2(l_own - m), v_aug[own]))
        else:
            acc = _dot(jnp.exp2(l_own - m), v_aug[own])
        o_ref[own, :] = (acc[:, :ATT_HEAD] / acc[:, ATT_HEAD:ATT_HEAD + 1]).astype(BF16)


def _moba(q, kv, batch):
    n, dq = q.shape
    t = n // batch
    group = dq // ATT_HEAD // KV_HEADS
    return pl.pallas_call(
        _moba_kernel, grid=(batch, KV_HEADS, group),
        in_specs=[pl.BlockSpec((t, ATT_HEAD), lambda b, kh, g: (b, kh * group + g)),
                  pl.BlockSpec((t, ATT_HEAD), lambda b, kh, g: (b, kh)),
                  pl.BlockSpec((t, ATT_HEAD), lambda b, kh, g: (b, KV_HEADS + kh))],
        out_specs=pl.BlockSpec((t, ATT_HEAD), lambda b, kh, g: (b, kh * group + g)),
        out_shape=jax.ShapeDtypeStruct((n, dq), BF16),
        compiler_params=_cparams(("parallel", "parallel", "arbitrary")))(q, kv, kv)


def _pad_lora(w_in, w_out):
    r = w_in.shape[1]
    rp = -(-r // LANES) * LANES
    return (jnp.pad(w_in, ((0, 0), (0, rp - r))).astype(BF16),
            jnp.pad(w_out, ((0, rp - r), (0, 0))).astype(BF16))


def _rope_tables(t):
    half = ATT_HEAD // 2
    inv = ROPE_THETA ** (-jnp.arange(half, dtype=F32) / half)
    ang = jnp.arange(t, dtype=F32)[:, None] * inv[None, :]
    cos, sin = jnp.cos(ang), jnp.sin(ang)
    return jnp.concatenate([cos, cos], axis=-1), jnp.concatenate([-sin, sin], axis=-1)


def kernel(x, ln_mix_g, ln_ffn_g, w_ff1, w_ff2, rw_mu, rw_w_rkv, rw_w0, rw_w1, rw_w2, rw_a0, rw_a1, rw_a2, rw_g1, rw_g2, rw_k_k, rw_k_a, rw_r_k, rw_gn_w, rw_gn_b, rw_w_o, rw_v0, rw_v1, rw_v2, kv_norm_g, w_kv, mb_w_q, mb_w_o, final_g):
    batch, t, d = x.shape
    n = batch * t
    depth = ln_mix_g.shape[0]
    n_rwkv = rw_mu.shape[0]
    h = x.reshape(n, d)
    cos, sin = _rope_tables(t)
    v_first = None
    kv = None
    for layer in range(depth):
        if layer < n_rwkv:
            i = layer
            w1, w2 = _pad_lora(rw_w1[i], rw_w2[i])
            a1, a2 = _pad_lora(rw_a1[i], rw_a2[i])
            g1, g2 = _pad_lora(rw_g1[i], rw_g2[i])
            v_lora = None
            if i > 0:
                v1, v2 = _pad_lora(rw_v1[i - 1], rw_v2[i - 1])
                v_lora = (rw_v0[i - 1], v1, v2)
            prep = _rwkv_prep(h, t, ln_mix_g[layer], rw_mu[i], rw_w0[i], w1, w2,
                              rw_a0[i], a1, a2, g1, g2, v_lora)
            x3, lw, a, gate = prep[:4]
            v_gate = prep[4] if i > 0 else None
            rkv = _mm3(x3, rw_w_rkv[i].astype(BF16))
            yg = _wkv(rkv, lw, a, gate, v_first, v_gate, rw_k_k[i], rw_k_a[i], rw_r_k[i],
                      rw_gn_w[i], rw_gn_b[i], batch)
            if i == 0:
                v_first = rkv
            h = _mm_res(yg, rw_w_o[i].astype(BF16), h)
        else:
            j = layer - n_rwkv
            if kv is None:
                kv = _norm_mm_rope(h, kv_norm_g, w_kv.astype(BF16), cos, sin,
                                   KV_HEADS * ATT_HEAD, t)
            q = _norm_mm_rope(h, ln_mix_g[layer], mb_w_q[j].astype(BF16), cos, sin,
                              mb_w_q.shape[2], t)
            att = _moba(q, kv, batch)
            h = _mm_res(att, mb_w_o[j].astype(BF16), h)
        h = _mlp(h, ln_ffn_g[layer], w_ff1[layer].astype(BF16), w_ff2[layer].astype(BF16),
                 final_g, layer == depth - 1)
    return h.reshape(batch, t, d)
```

```python
import functools
import math

import jax
import jax.numpy as jnp
from jax import lax
from jax.experimental import pallas as pl
from jax.experimental.pallas import tpu as pltpu

F32 = jnp.float32
BF16 = jnp.bfloat16

RMS_EPS = 1e-6
GN_EPS = 64e-5
RWKV_HEAD = 64
ATT_HEAD = 128
KV_HEADS = 4
MOBA_BLOCK = 256
MOBA_TOPK = 3
ROPE_THETA = 10000.0
LANES = 128
WKV_CHUNK = 64
WKV_CHAIN = 10
WKV_SUBCHUNKS = 2
MASKED_LOGIT = -1e30
VMEM_LIMIT = 56 * 1024 * 1024


def _cparams(sem):
    return pltpu.CompilerParams(dimension_semantics=sem, vmem_limit_bytes=VMEM_LIMIT)


def _rms(x, g):
    return x * lax.rsqrt(jnp.mean(x * x, axis=-1, keepdims=True) + RMS_EPS) * g


def _dot(a, b):
    return jnp.dot(a.astype(BF16), b.astype(BF16), preferred_element_type=F32)


def _dot_nt(a, b):
    return lax.dot_general(a.astype(BF16), b.astype(BF16), (((1,), (1,)), ((), ())),
                           preferred_element_type=F32)


def _dot_tn(a, b):
    return lax.dot_general(a.astype(BF16), b.astype(BF16), (((0,), (0,)), ((), ())),
                           preferred_element_type=F32)


def _split3(x):
    h1 = x.astype(BF16)
    r1 = x - h1.astype(F32)
    h2 = r1.astype(BF16)
    h3 = (r1 - h2.astype(F32)).astype(BF16)
    return h1, h2, h3


def _rwkv_prep_kernel(*refs, tiles_per_seq, has_v):
    (h_ref, hprev_ref, g_ref, mu_ref, w0_ref, w1_ref, w2_ref, a0_ref, a1_ref, a2_ref,
     g1_ref, g2_ref) = refs[:12]
    if has_v:
        v0_ref, v1_ref, v2_ref = refs[12:15]
        x3_ref, lw_ref, a_ref, gate_ref, vg_ref = refs[15:]
    else:
        x3_ref, lw_ref, a_ref, gate_ref = refs[12:]
    i = pl.program_id(0)
    g = g_ref[...]
    hn = _rms(h_ref[...], g)
    prev = _rms(hprev_ref[7:8, :], g)
    prev = jnp.where(i % tiles_per_seq == 0, 0.0, prev)
    rows = lax.broadcasted_iota(jnp.int32, hn.shape, 0)
    shifted = jnp.where(rows == 0, prev, pltpu.roll(hn, 1, axis=0))
    xx = shifted - hn
    x3_ref[0] = (hn + xx * mu_ref[0:1, :]).astype(BF16)
    x3_ref[1] = (hn + xx * mu_ref[2:3, :]).astype(BF16)
    xv = (hn + xx * mu_ref[3:4, :]).astype(BF16)
    x3_ref[2] = xv
    xw = hn + xx * mu_ref[1:2, :]
    w = w0_ref[...] + _dot(jnp.tanh(_dot(xw, w1_ref[...])), w2_ref[...])
    w = -jax.nn.softplus(-w) - 0.5
    lw_ref[...] = -jnp.exp(w)
    xa = hn + xx * mu_ref[4:5, :]
    a_ref[...] = jax.nn.sigmoid(a0_ref[...] + _dot(_dot(xa, a1_ref[...]), a2_ref[...]))
    xg = hn + xx * mu_ref[5:6, :]
    gate_ref[...] = _dot(jax.nn.sigmoid(_dot(xg, g1_ref[...])), g2_ref[...])
    if has_v:
        vg_ref[...] = jax.nn.sigmoid(v0_ref[...] + _dot(_dot(xv, v1_ref[...]), v2_ref[...]))


def _rwkv_prep(h, seq_len, g, mu, w0, w1, w2, a0, a1, a2, g1, g2, v_lora, tm=256):
    n, d = h.shape
    has_v = v_lora is not None
    row = lambda v: v.reshape(1, d)
    full = lambda arr: pl.BlockSpec(arr.shape, lambda i: (0,) * arr.ndim)
    tile = pl.BlockSpec((tm, d), lambda i: (i, 0))
    ins = [h, h, row(g), mu, row(w0), w1, w2, row(a0), a1, a2, g1, g2]
    in_specs = [tile, pl.BlockSpec((8, d), lambda i: (jnp.maximum(i * (tm // 8) - 1, 0), 0))]
    in_specs += [full(x) for x in ins[2:]]
    out_shape = [jax.ShapeDtypeStruct((3, n, d), BF16)] + [jax.ShapeDtypeStruct((n, d), F32)] * 3
    out_specs = [pl.BlockSpec((3, tm, d), lambda i: (0, i, 0)), tile, tile, tile]
    if has_v:
        v0, v1, v2 = v_lora
        extra = [row(v0), v1, v2]
        ins += extra
        in_specs += [full(x) for x in extra]
        out_shape.append(jax.ShapeDtypeStruct((n, d), F32))
        out_specs.append(tile)
    return pl.pallas_call(
        functools.partial(_rwkv_prep_kernel, tiles_per_seq=seq_len // tm, has_v=has_v),
        grid=(n // tm,), in_specs=in_specs, out_specs=out_specs, out_shape=out_shape,
        compiler_params=_cparams(("parallel",)))(*ins)


def _mm3_kernel(x_ref, w_ref, o_ref):
    o_ref[...] = jnp.dot(x_ref[...], w_ref[...], preferred_element_type=F32)


def _mm3(x3, w4, layer, tm=1024, tn=1024):
    p, n, k = x3.shape
    m = w4.shape[3]
    tm, tn = min(tm, n), min(tn, m)
    return pl.pallas_call(
        _mm3_kernel, grid=(p, n // tm, m // tn),
        in_specs=[pl.BlockSpec((None, tm, k), lambda q, i, j: (q, i, 0)),
                  pl.BlockSpec((None, None, k, tn), lambda q, i, j: (layer, q, 0, j))],
        out_specs=pl.BlockSpec((None, tm, tn), lambda q, i, j: (q, i, j)),
        out_shape=jax.ShapeDtypeStruct((p, n, m), F32),
        compiler_params=_cparams(("parallel", "parallel", "arbitrary")))(x3, w4)


def _mm_res_kernel(x_ref, w_ref, r_ref, o_ref):
    o_ref[...] = r_ref[...] + jnp.dot(x_ref[...], w_ref[...], preferred_element_type=F32)


def _mm_res(x, w3, layer, res, tm=1024, tn=1024):
    n, k = x.shape
    m = w3.shape[2]
    tm, tn = min(tm, n), min(tn, m)
    return pl.pallas_call(
        _mm_res_kernel, grid=(n // tm, m // tn),
        in_specs=[pl.BlockSpec((tm, k), lambda i, j: (i, 0)),
                  pl.BlockSpec((None, k, tn), lambda i, j: (layer, 0, j)),
                  pl.BlockSpec((tm, tn), lambda i, j: (i, j))],
        out_specs=pl.BlockSpec((tm, tn), lambda i, j: (i, j)),
        out_shape=jax.ShapeDtypeStruct((n, m), F32),
        compiler_params=_cparams(("parallel", "arbitrary")))(x, w3, res)


def _norm_mm_rope_kernel(h_ref, g_ref, w_ref, cos_ref, sin_ref, o_ref, hn_ref, *, rope_tiles):
    j = pl.program_id(1)

    @pl.when(j == 0)
    def _():
        hn_ref[...] = _rms(h_ref[...], g_ref[...]).astype(BF16)

    acc = jnp.dot(hn_ref[...], w_ref[...], preferred_element_type=F32)

    @pl.when(j < rope_tiles)
    def _():
        cos, sin = cos_ref[...], sin_ref[...]
        for hh in range(acc.shape[1] // ATT_HEAD):
            x = acc[:, hh * ATT_HEAD:(hh + 1) * ATT_HEAD]
            o_ref[:, hh * ATT_HEAD:(hh + 1) * ATT_HEAD] = (
                x * cos + pltpu.roll(x, ATT_HEAD // 2, axis=1) * sin)

    @pl.when(j >= rope_tiles)
    def _():
        o_ref[...] = acc


def _norm_mm_rope(h, g, w3, layer, cos, sin, rope_cols, seq_len, tm=1024, tn=1024):
    n, d = h.shape
    m = w3.shape[2]
    tm, tn = min(tm, seq_len), min(tn, rope_cols)
    return pl.pallas_call(
        functools.partial(_norm_mm_rope_kernel, rope_tiles=rope_cols // tn),
        grid=(n // tm, m // tn),
        in_specs=[pl.BlockSpec((tm, d), lambda i, j: (i, 0)),
                  pl.BlockSpec((1, d), lambda i, j: (0, 0)),
                  pl.BlockSpec((None, d, tn), lambda i, j: (layer, 0, j)),
                  pl.BlockSpec((tm, ATT_HEAD), lambda i, j: (i % (seq_len // tm), 0)),
                  pl.BlockSpec((tm, ATT_HEAD), lambda i, j: (i % (seq_len // tm), 0))],
        out_specs=pl.BlockSpec((tm, tn), lambda i, j: (i, j)),
        out_shape=jax.ShapeDtypeStruct((n, m), F32),
        scratch_shapes=[pltpu.VMEM((tm, d), BF16)],
        compiler_params=_cparams(("parallel", "arbitrary")))(h, g.reshape(1, d), w3, cos, sin)


def _mlp_kernel(h_ref, g_ref, w1_ref, w2_ref, gf_ref, o_ref, hn_ref, *, final_norm):
    j = pl.program_id(1)

    @pl.when(j == 0)
    def _():
        h = h_ref[...]
        hn_ref[...] = _rms(h, g_ref[...]).astype(BF16)
        o_ref[...] = h

    u = jnp.dot(hn_ref[...], w1_ref[...], preferred_element_type=F32)
    u = jnp.square(jnp.maximum(u, 0.0)).astype(BF16)
    o_ref[...] += jnp.dot(u, w2_ref[...], preferred_element_type=F32)

    if final_norm:
        @pl.when(j == pl.num_programs(1) - 1)
        def _():
            o_ref[...] = _rms(o_ref[...], gf_ref[...])


def _mlp(h, g, w1, w2, layer, gf, final_norm, tm=1024, tf=512):
    n, d = h.shape
    f = w1.shape[2]
    tm = min(tm, n)
    return pl.pallas_call(
        functools.partial(_mlp_kernel, final_norm=final_norm),
        grid=(n // tm, f // tf),
        in_specs=[pl.BlockSpec((tm, d), lambda i, j: (i, 0)),
                  pl.BlockSpec((1, d), lambda i, j: (0, 0)),
                  pl.BlockSpec((None, d, tf), lambda i, j: (layer, 0, j)),
                  pl.BlockSpec((None, tf, d), lambda i, j: (layer, j, 0)),
                  pl.BlockSpec((1, d), lambda i, j: (0, 0))],
        out_specs=pl.BlockSpec((tm, d), lambda i, j: (i, 0)),
        out_shape=jax.ShapeDtypeStruct((n, d), F32),
        scratch_shapes=[pltpu.VMEM((tm, d), BF16)],
        compiler_params=_cparams(("parallel", "arbitrary")))(
            h, g.reshape(1, d), w1, w2, gf.reshape(1, d))


def _wkv_kernel(*refs, has_v):
    r_ref, k_ref, v_ref, lw_ref, a_ref, gate_ref = refs[:6]
    nxt = 6
    if has_v:
        vf_ref, vg_ref = refs[6:8]
        nxt = 8
    kk_ref, ka_ref, rk_ref, gnw_ref, gnb_ref, o_ref, st_ref = refs[nxt:]

    @pl.when(pl.program_id(2) == 0)
    def _():
        st_ref[...] = jnp.zeros_like(st_ref)

    L = WKV_CHUNK
    n2 = 2 * L
    half = RWKV_HEAD
    m0 = lax.broadcasted_iota(jnp.int32, (L, LANES), 1) < half
    trow = lax.broadcasted_iota(jnp.int32, (L, LANES), 0)
    zeros = jnp.zeros((L, LANES), F32)

    def h0(x):
        return jnp.where(m0, x, 0.0)

    def h1(x):
        return jnp.where(m0, 0.0, x)

    def seg_sum(x):
        s0 = jnp.sum(h0(x), axis=-1, keepdims=True)
        s1 = jnp.sum(h1(x), axis=-1, keepdims=True)
        return jnp.where(m0, s0, s1)

    def vcat(*xs):
        return jnp.concatenate(xs, axis=0)

    row = lax.broadcasted_iota(jnp.int32, (n2, n2), 0)
    col = lax.broadcasted_iota(jnp.int32, (n2, n2), 1)
    keep = (col % L) < (row % L) + jnp.where(row < L, 0, 1)
    same_head = (row // L) == (col // L)
    steps = int(math.log2(L))

    def pair_stages(sub, p):
        sl = slice(p * LANES, (p + 1) * LANES)
        rs = slice(sub * L, (sub + 1) * L)
        r, k, v, lw, a = r_ref[rs, sl], k_ref[rs, sl], v_ref[rs, sl], lw_ref[rs, sl], a_ref[rs, sl]
        if has_v:
            v = v + (vf_ref[rs, sl] - v) * vg_ref[rs, sl]
        kq = k * kk_ref[:, sl]
        kk = kq * lax.rsqrt(jnp.maximum(seg_sum(kq * kq), 1e-24))
        k2 = k * (1.0 + (a - 1.0) * ka_ref[:, sl])
        b = kk * a
        c = lw
        step = 1
        while step < L:
            c = c + jnp.where(trow >= step, pltpu.roll(c, step, axis=0), 0.0)
            step *= 2
        c_last = c[L - 1:L, :]
        e_neg = jnp.exp(-c)
        e_last = jnp.exp(c_last - c)
        at, rt, bt, kt = -kk * jnp.exp(c - lw), r * jnp.exp(c), b * e_neg, k2 * e_neg
        vr = pltpu.roll(v, half, axis=1)
        bk = vcat(b * e_last, k2 * e_last)
        yield
        a0 = jnp.where(keep, _dot_nt(vcat(h0(at), h0(rt)), vcat(bt, kt)), 0.0)
        a1 = jnp.where(keep, _dot_nt(vcat(h1(at), h1(rt)), vcat(kt, bt)), 0.0)
        yield
        x = vcat(h0(at) + _dot(a0[:L], vcat(zeros, h1(vr))),
                 h1(at) + _dot(a1[:L], vcat(h0(vr), zeros)))
        pcat = jnp.where(m0, a0[:L], a1[:L])
        yield
        for s in range(steps):
            pbd = vcat(h0(pcat), h1(pcat))
            x = x + _dot(pbd, x)
            if s + 1 < steps:
                pcat = _dot(pcat, pbd)
            yield
        wcat = jnp.where(m0, x[:L], x[L:])
        u0 = pltpu.roll(jnp.where(m0, x[L:], x[:L]), half, axis=1)
        wh = _dot_nt(vcat(wcat, rt), st_ref[p])
        yield
        u = wh[:L] + u0
        y = wh[L:] + _dot(jnp.concatenate([a0[L:], a1[L:]], axis=1),
                          vcat(h0(u), h0(v), h1(v), h1(u)))
        st_ref[p] = st_ref[p] * jnp.exp(c_last) + jnp.where(
            same_head, _dot_tn(vcat(u, v), bk), 0.0)
        yield
        mean = seg_sum(y) * (1.0 / half)
        dlt = y - mean
        var = seg_sum(dlt * dlt) * (1.0 / half)
        yn = dlt * lax.rsqrt(var + GN_EPS) * gnw_ref[:, sl] + gnb_ref[:, sl]
        bonus = seg_sum(r * k2 * rk_ref[:, sl]) * v
        o_ref[rs, sl] = ((yn + bonus) * gate_ref[rs, sl]).astype(BF16)

    n_pairs = o_ref.shape[1] // LANES
    n_sub = o_ref.shape[0] // L
    plan = {}
    for sub in range(n_sub):
        t0 = sub * WKV_CHAIN
        for p in range(n_pairs):
            spread = 1 + (p * WKV_CHAIN) // n_pairs
            first = t0 if sub == 0 else t0 - WKV_CHAIN + spread
            last = t0 + WKV_CHAIN + (1 if sub == n_sub - 1 else spread)
            gen = pair_stages(sub, p)
            for order, tick in enumerate([first] + [t0 + s for s in range(1, WKV_CHAIN + 1)] + [last]):
                plan.setdefault(tick, []).append((order in (0, WKV_CHAIN + 1), sub, p, gen))
    for tick in sorted(plan):
        for _, _, _, gen in sorted(plan[tick], key=lambda e: e[:3]):
            next(gen, None)


def _wkv(rkv, lw, a, gate, v_first, v_gate, k_k, k_a, r_k, gn_w, gn_b, batch, width=2048):
    _, n, d = rkv.shape
    width = min(width, d)
    L = WKV_CHUNK * WKV_SUBCHUNKS
    nc = n // batch // L
    has_v = v_first is not None
    tile = pl.BlockSpec((L, width), lambda b, p, c: (b * nc + c, p))
    rkv_spec = lambda q: pl.BlockSpec((None, L, width), lambda b, p, c: (q, b * nc + c, p))
    prm = pl.BlockSpec((1, width), lambda b, p, c: (0, p))
    ins = [rkv, rkv, rkv, lw, a, gate]
    in_specs = [rkv_spec(0), rkv_spec(1), rkv_spec(2), tile, tile, tile]
    if has_v:
        ins += [v_first, v_gate]
        in_specs += [rkv_spec(2), tile]
    ins += [x.reshape(1, d) for x in (k_k, k_a, r_k, gn_w, gn_b)]
    in_specs += [prm] * 5
    return pl.pallas_call(
        functools.partial(_wkv_kernel, has_v=has_v),
        grid=(batch, d // width, nc), in_specs=in_specs, out_specs=tile,
        out_shape=jax.ShapeDtypeStruct((n, d), BF16),
        scratch_shapes=[pltpu.VMEM((width // LANES, LANES, LANES), F32)],
        compiler_params=_cparams(("parallel", "parallel", "arbitrary")))(*ins)


def _moba_kernel(q_ref, k_ref, v_ref, o_ref):
    bs = MOBA_BLOCK
    t = k_ref.shape[0]
    nb = t // bs
    nbp = 16
    qscale = (ATT_HEAD ** -0.5) * math.log2(math.e)
    kf = k_ref[...]
    kmean = jnp.mean(kf.reshape(nb, bs, ATT_HEAD), axis=1)
    kmean = jnp.concatenate([kmean, jnp.zeros((nbp - nb, ATT_HEAD), F32)], axis=0)
    km1, km2, _ = _split3(kmean)
    key_blk = lax.broadcasted_iota(jnp.int32, (t, LANES), 0) // bs
    key_lane = lax.broadcasted_iota(jnp.int32, (t, LANES), 1)
    k_aug = jnp.concatenate(
        [kf.astype(BF16), jnp.where(key_blk == key_lane, MASKED_LOGIT, 0.0).astype(BF16)], axis=1)
    v_aug = jnp.concatenate(
        [v_ref[...].astype(BF16), jnp.where(key_lane == 0, 1.0, 0.0).astype(BF16)], axis=1)
    blk = lax.broadcasted_iota(jnp.int32, (nbp, bs), 0)
    eye = (lax.broadcasted_iota(jnp.int32, (nbp, LANES), 0)
           == lax.broadcasted_iota(jnp.int32, (nbp, LANES), 1)).astype(BF16)
    qi = lax.broadcasted_iota(jnp.int32, (bs, bs), 0)
    ki = lax.broadcasted_iota(jnp.int32, (bs, bs), 1)
    causal_bias = jnp.where(ki <= qi, 0.0, MASKED_LOGIT)
    for qb in range(nb):
        q = q_ref[qb * bs:(qb + 1) * bs, :]
        qs = (q * qscale).astype(BF16)
        own = slice(qb * bs, (qb + 1) * bs)
        l_own = _dot_nt(qs, k_aug[own, :ATT_HEAD]) + causal_bias
        m = jnp.max(l_own, axis=-1, keepdims=True)
        if qb > 0:
            if qb > MOBA_TOPK:
                q1, q2, _ = _split3(q)
                gate = _dot_nt(km1, q1) + _dot_nt(km1, q2) + _dot_nt(km2, q1)
                cnt = jnp.zeros((nbp, bs), jnp.int32)
                for mb in range(qb):
                    gm = gate[mb:mb + 1, :]
                    beats = (gm > gate) | ((gm == gate) & (mb < blk))
                    cnt = cnt + beats.astype(jnp.int32)
                notsel = jnp.where((cnt >= MOBA_TOPK) & (blk < qb), 1.0, 0.0)
                flags = _dot_tn(notsel, eye).astype(BF16)
            else:
                flags = jnp.zeros((bs, LANES), BF16)
            l_past = _dot_nt(jnp.concatenate([qs, flags], axis=1), k_aug[:qb * bs])
            m = jnp.maximum(m, jnp.max(l_past, axis=-1, keepdims=True))
            acc = (_dot(jnp.exp2(l_past - m), v_aug[:qb * bs]) + _dot(jnp.exp2(l_own - m), v_aug[own]))
        else:
            acc = _dot(jnp.exp2(l_own - m), v_aug[own])
        o_ref[own, :] = (acc[:, :ATT_HEAD] / acc[:, ATT_HEAD:ATT_HEAD + 1]).astype(BF16)


def _moba(q, kv, batch):
    n, dq = q.shape
    t = n // batch
    group = dq // ATT_HEAD // KV_HEADS
    return pl.pallas_call(
        _moba_kernel, grid=(batch, KV_HEADS, group),
        in_specs=[pl.BlockSpec((t, ATT_HEAD), lambda b, kh, g: (b, kh * group + g)),
                  pl.BlockSpec((t, ATT_HEAD), lambda b, kh, g: (b, kh)),
                  pl.BlockSpec((t, ATT_HEAD), lambda b, kh, g: (b, KV_HEADS + kh))],
        out_specs=pl.BlockSpec((t, ATT_HEAD), lambda b, kh, g: (b, kh * group + g)),
        out_shape=jax.ShapeDtypeStruct((n, dq), BF16),
        compiler_params=_cparams(("parallel", "parallel", "arbitrary")))(q, kv, kv)


def _pad_lora(w_in, w_out):
    r = w_in.shape[1]
    rp = -(-r // LANES) * LANES
    return (jnp.pad(w_in, ((0, 0), (0, rp - r))).astype(BF16),
            jnp.pad(w_out, ((0, rp - r), (0, 0))).astype(BF16))


def _rope_tables(t):
    half = ATT_HEAD // 2
    inv = ROPE_THETA ** (-jnp.arange(half, dtype=F32) / half)
    ang = jnp.arange(t, dtype=F32)[:, None] * inv[None, :]
    cos, sin = jnp.cos(ang), jnp.sin(ang)
    return jnp.concatenate([cos, cos], axis=-1), jnp.concatenate([-sin, sin], axis=-1)


def kernel(x, ln_mix_g, ln_ffn_g, w_ff1, w_ff2, rw_mu, rw_w_rkv, rw_w0, rw_w1, rw_w2, rw_a0, rw_a1, rw_a2, rw_g1, rw_g2, rw_k_k, rw_k_a, rw_r_k, rw_gn_w, rw_gn_b, rw_w_o, rw_v0, rw_v1, rw_v2, kv_norm_g, w_kv, mb_w_q, mb_w_o, final_g):
    batch, t, d = x.shape
    n = batch * t
    depth = ln_mix_g.shape[0]
    n_rwkv = rw_mu.shape[0]
    h = x.reshape(n, d)
    cos, sin = _rope_tables(t)
    w_ff1, w_ff2, rw_w_rkv, rw_w_o, mb_w_q, mb_w_o = (
        w.astype(BF16) for w in (w_ff1, w_ff2, rw_w_rkv, rw_w_o, mb_w_q, mb_w_o))
    w_kv = w_kv.astype(BF16)[None]
    v_first = None
    kv = None
    for layer in range(depth):
        if layer < n_rwkv:
            i = layer
            w1, w2 = _pad_lora(rw_w1[i], rw_w2[i])
            a1, a2 = _pad_lora(rw_a1[i], rw_a2[i])
            g1, g2 = _pad_lora(rw_g1[i], rw_g2[i])
            v_lora = None
            if i > 0:
                v1, v2 = _pad_lora(rw_v1[i - 1], rw_v2[i - 1])
                v_lora = (rw_v0[i - 1], v1, v2)
            prep = _rwkv_prep(h, t, ln_mix_g[layer], rw_mu[i], rw_w0[i], w1, w2,
                              rw_a0[i], a1, a2, g1, g2, v_lora)
            x3, lw, a, gate = prep[:4]
            v_gate = prep[4] if i > 0 else None
            rkv = _mm3(x3, rw_w_rkv, i)
            yg = _wkv(rkv, lw, a, gate, v_first, v_gate, rw_k_k[i], rw_k_a[i], rw_r_k[i],
                      rw_gn_w[i], rw_gn_b[i], batch)
            if i == 0:
                v_first = rkv
            h = _mm_res(yg, rw_w_o, i, h)
        else:
            j = layer - n_rwkv
            if kv is None:
                kv = _norm_mm_rope(h, kv_norm_g, w_kv, 0, cos, sin, KV_HEADS * ATT_HEAD, t)
            q = _norm_mm_rope(h, ln_mix_g[layer], mb_w_q, j, cos, sin, mb_w_q.shape[2], t)
            att = _moba(q, kv, batch)
            h = _mm_res(att, mb_w_o, j, h)
        h = _mlp(h, ln_ffn_g[layer], w_ff1, w_ff2, layer, final_g, layer == depth - 1)
    return h.reshape(batch, t, d)
```

```python
import functools
import math

import jax
import jax.numpy as jnp
from jax import lax
from jax.experimental import pallas as pl
from jax.experimental.pallas import tpu as pltpu

F32 = jnp.float32
BF16 = jnp.bfloat16

RMS_EPS = 1e-6
GN_EPS = 64e-5
RWKV_HEAD = 64
ATT_HEAD = 128
KV_HEADS = 4
MOBA_BLOCK = 256
MOBA_TOPK = 3
ROPE_THETA = 10000.0
LANES = 128
WKV_CHUNK = 64
WKV_CHAIN = 10
WKV_SUBCHUNKS = 2
MASKED_LOGIT = -1e30
VMEM_LIMIT = 56 * 1024 * 1024


def _cparams(sem):
    return pltpu.CompilerParams(dimension_semantics=sem, vmem_limit_bytes=VMEM_LIMIT)


def _rms(x, g):
    return x * lax.rsqrt(jnp.mean(x * x, axis=-1, keepdims=True) + RMS_EPS) * g


def _sigmoid(x):
    return 0.5 * jnp.tanh(0.5 * x) + 0.5


def _dot(a, b):
    return jnp.dot(a.astype(BF16), b.astype(BF16), preferred_element_type=F32)


def _dot_nt(a, b):
    return lax.dot_general(a.astype(BF16), b.astype(BF16), (((1,), (1,)), ((), ())),
                           preferred_element_type=F32)


def _dot_tn(a, b):
    return lax.dot_general(a.astype(BF16), b.astype(BF16), (((0,), (0,)), ((), ())),
                           preferred_element_type=F32)


def _split3(x):
    h1 = x.astype(BF16)
    r1 = x - h1.astype(F32)
    h2 = r1.astype(BF16)
    h3 = (r1 - h2.astype(F32)).astype(BF16)
    return h1, h2, h3


def _rwkv_prep_kernel(*refs, tiles_per_seq, has_v):
    (h_ref, hprev_ref, g_ref, mu_ref, w0_ref, w1_ref, w2_ref, a0_ref, a1_ref, a2_ref,
     g1_ref, g2_ref) = refs[:12]
    if has_v:
        v0_ref, v1_ref, v2_ref = refs[12:15]
        x3_ref, lw_ref, a_ref, gate_ref, vg_ref = refs[15:]
    else:
        x3_ref, lw_ref, a_ref, gate_ref = refs[12:]
    i = pl.program_id(0)
    g = g_ref[...]
    hn = _rms(h_ref[...], g)
    prev = _rms(hprev_ref[7:8, :], g)
    prev = jnp.where(i % tiles_per_seq == 0, 0.0, prev)
    rows = lax.broadcasted_iota(jnp.int32, hn.shape, 0)
    shifted = jnp.where(rows == 0, prev, pltpu.roll(hn, 1, axis=0))
    xx = shifted - hn
    x3_ref[0] = (hn + xx * mu_ref[0:1, :]).astype(BF16)
    x3_ref[1] = (hn + xx * mu_ref[2:3, :]).astype(BF16)
    xv = (hn + xx * mu_ref[3:4, :]).astype(BF16)
    x3_ref[2] = xv
    xw = hn + xx * mu_ref[1:2, :]
    z = w0_ref[...] + _dot(jnp.tanh(_dot(xw, w1_ref[...])), w2_ref[...])
    lw_ref[...] = -math.exp(-0.5) * _sigmoid(z)
    xa = hn + xx * mu_ref[4:5, :]
    a_ref[...] = _sigmoid(a0_ref[...] + _dot(_dot(xa, a1_ref[...]), a2_ref[...]))
    xg = hn + xx * mu_ref[5:6, :]
    gate_ref[...] = _dot(_sigmoid(_dot(xg, g1_ref[...])), g2_ref[...])
    if has_v:
        vg_ref[...] = _sigmoid(v0_ref[...] + _dot(_dot(xv, v1_ref[...]), v2_ref[...]))


def _rwkv_prep(h, seq_len, g, mu, w0, w1, w2, a0, a1, a2, g1, g2, v_lora, tm=256):
    n, d = h.shape
    has_v = v_lora is not None
    row = lambda v: v.reshape(1, d)
    full = lambda arr: pl.BlockSpec(arr.shape, lambda i: (0,) * arr.ndim)
    tile = pl.BlockSpec((tm, d), lambda i: (i, 0))
    ins = [h, h, row(g), mu, row(w0), w1, w2, row(a0), a1, a2, g1, g2]
    in_specs = [tile, pl.BlockSpec((8, d), lambda i: (jnp.maximum(i * (tm // 8) - 1, 0), 0))]
    in_specs += [full(x) for x in ins[2:]]
    out_shape = [jax.ShapeDtypeStruct((3, n, d), BF16)] + [jax.ShapeDtypeStruct((n, d), F32)] * 3
    out_specs = [pl.BlockSpec((3, tm, d), lambda i: (0, i, 0)), tile, tile, tile]
    if has_v:
        v0, v1, v2 = v_lora
        extra = [row(v0), v1, v2]
        ins += extra
        in_specs += [full(x) for x in extra]
        out_shape.append(jax.ShapeDtypeStruct((n, d), F32))
        out_specs.append(tile)
    return pl.pallas_call(
        functools.partial(_rwkv_prep_kernel, tiles_per_seq=seq_len // tm, has_v=has_v),
        grid=(n // tm,), in_specs=in_specs, out_specs=out_specs, out_shape=out_shape,
        compiler_params=_cparams(("parallel",)))(*ins)


def _mm3_kernel(x_ref, w_ref, o_ref):
    o_ref[...] = jnp.dot(x_ref[...], w_ref[...], preferred_element_type=F32)


def _mm3(x3, w4, layer, tm=1024, tn=1024):
    p, n, k = x3.shape
    m = w4.shape[3]
    tm, tn = min(tm, n), min(tn, m)
    return pl.pallas_call(
        _mm3_kernel, grid=(p, n // tm, m // tn),
        in_specs=[pl.BlockSpec((None, tm, k), lambda q, i, j: (q, i, 0)),
                  pl.BlockSpec((None, None, k, tn), lambda q, i, j: (layer, q, 0, j))],
        out_specs=pl.BlockSpec((None, tm, tn), lambda q, i, j: (q, i, j)),
        out_shape=jax.ShapeDtypeStruct((p, n, m), F32),
        compiler_params=_cparams(("parallel", "parallel", "arbitrary")))(x3, w4)


def _mm_res_kernel(x_ref, w_ref, r_ref, o_ref):
    o_ref[...] = r_ref[...] + jnp.dot(x_ref[...], w_ref[...], preferred_element_type=F32)


def _mm_res(x, w3, layer, res, tm=1024, tn=1024):
    n, k = x.shape
    m = w3.shape[2]
    tm, tn = min(tm, n), min(tn, m)
    return pl.pallas_call(
        _mm_res_kernel, grid=(n // tm, m // tn),
        in_specs=[pl.BlockSpec((tm, k), lambda i, j: (i, 0)),
                  pl.BlockSpec((None, k, tn), lambda i, j: (layer, 0, j)),
                  pl.BlockSpec((tm, tn), lambda i, j: (i, j))],
        out_specs=pl.BlockSpec((tm, tn), lambda i, j: (i, j)),
        out_shape=jax.ShapeDtypeStruct((n, m), F32),
        compiler_params=_cparams(("parallel", "arbitrary")))(x, w3, res)


def _norm_mm_rope_kernel(h_ref, g_ref, w_ref, cos_ref, sin_ref, o_ref, hn_ref, *, rope_tiles):
    j = pl.program_id(1)

    @pl.when(j == 0)
    def _():
        hn_ref[...] = _rms(h_ref[...], g_ref[...]).astype(BF16)

    acc = jnp.dot(hn_ref[...], w_ref[...], preferred_element_type=F32)

    @pl.when(j < rope_tiles)
    def _():
        cos, sin = cos_ref[...], sin_ref[...]
        for hh in range(acc.shape[1] // ATT_HEAD):
            x = acc[:, hh * ATT_HEAD:(hh + 1) * ATT_HEAD]
            o_ref[:, hh * ATT_HEAD:(hh + 1) * ATT_HEAD] = (
                x * cos + pltpu.roll(x, ATT_HEAD // 2, axis=1) * sin)

    @pl.when(j >= rope_tiles)
    def _():
        o_ref[...] = acc


def _norm_mm_rope(h, g, w3, layer, cos, sin, rope_cols, seq_len, tm=1024, tn=1024):
    n, d = h.shape
    m = w3.shape[2]
    tm, tn = min(tm, seq_len), min(tn, rope_cols)
    return pl.pallas_call(
        functools.partial(_norm_mm_rope_kernel, rope_tiles=rope_cols // tn),
        grid=(n // tm, m // tn),
        in_specs=[pl.BlockSpec((tm, d), lambda i, j: (i, 0)),
                  pl.BlockSpec((1, d), lambda i, j: (0, 0)),
                  pl.BlockSpec((None, d, tn), lambda i, j: (layer, 0, j)),
                  pl.BlockSpec((tm, ATT_HEAD), lambda i, j: (i % (seq_len // tm), 0)),
                  pl.BlockSpec((tm, ATT_HEAD), lambda i, j: (i % (seq_len // tm), 0))],
        out_specs=pl.BlockSpec((tm, tn), lambda i, j: (i, j)),
        out_shape=jax.ShapeDtypeStruct((n, m), F32),
        scratch_shapes=[pltpu.VMEM((tm, d), BF16)],
        compiler_params=_cparams(("parallel", "arbitrary")))(h, g.reshape(1, d), w3, cos, sin)


def _mlp_kernel(*refs, final_norm, n_cast):
    h_ref, g_ref, w1_ref, w2_ref, gf_ref = refs[:5]
    src_refs = refs[5:5 + n_cast]
    o_ref = refs[5 + n_cast]
    dst_refs = refs[6 + n_cast:6 + 2 * n_cast]
    hn_ref = refs[6 + 2 * n_cast]
    for src, dst in zip(src_refs, dst_refs):
        dst[...] = src[...].astype(BF16)
    j = pl.program_id(1)

    @pl.when(j == 0)
    def _():
        h = h_ref[...]
        hn_ref[...] = _rms(h, g_ref[...]).astype(BF16)
        o_ref[...] = h

    u = jnp.dot(hn_ref[...], w1_ref[...], preferred_element_type=F32)
    u = jnp.square(jnp.maximum(u, 0.0)).astype(BF16)
    o_ref[...] += jnp.dot(u, w2_ref[...], preferred_element_type=F32)

    if final_norm:
        @pl.when(j == pl.num_programs(1) - 1)
        def _():
            o_ref[...] = _rms(o_ref[...], gf_ref[...])


def _cast_tile(rows, cols, gi, gj):
    if (rows // gi) % 16 == 0 and (cols // gj) % LANES == 0:
        return (rows // gi, cols // gj), lambda i, j: (i, j)
    assert (rows // gj) % 16 == 0 and (cols // gi) % LANES == 0, (rows, cols, gi, gj)
    return (rows // gj, cols // gi), lambda i, j: (j, i)


def _mlp(h, g, w1, w2, layer, gf, final_norm, cast_jobs=(), tm=1024, tf=512):
    n, d = h.shape
    f = w1.shape[2]
    tm = min(tm, n)
    gi, gj = n // tm, f // tf
    ins = [h, g.reshape(1, d), w1, w2, gf.reshape(1, d)]
    in_specs = [pl.BlockSpec((tm, d), lambda i, j: (i, 0)),
                pl.BlockSpec((1, d), lambda i, j: (0, 0)),
                pl.BlockSpec((None, d, tf), lambda i, j: (layer, 0, j)),
                pl.BlockSpec((None, tf, d), lambda i, j: (layer, j, 0)),
                pl.BlockSpec((1, d), lambda i, j: (0, 0))]
    out_shape = [jax.ShapeDtypeStruct((n, d), F32)]
    out_specs = [pl.BlockSpec((tm, d), lambda i, j: (i, 0))]
    for w, idx in cast_jobs:
        _, rows, cols = w.shape
        blk, tile_index = _cast_tile(rows, cols, gi, gj)
        ins.append(w)
        in_specs.append(pl.BlockSpec((None,) + blk,
                                     lambda i, j, idx=idx, t=tile_index: (idx,) + t(i, j)))
        out_shape.append(jax.ShapeDtypeStruct((1, rows, cols), BF16))
        out_specs.append(pl.BlockSpec((None,) + blk, lambda i, j, t=tile_index: (0,) + t(i, j)))
    return pl.pallas_call(
        functools.partial(_mlp_kernel, final_norm=final_norm, n_cast=len(cast_jobs)),
        grid=(gi, gj), in_specs=in_specs, out_specs=out_specs, out_shape=out_shape,
        scratch_shapes=[pltpu.VMEM((tm, d), BF16)],
        compiler_params=_cparams(("parallel", "arbitrary")))(*ins)


def _wkv_kernel(*refs, has_v):
    r_ref, k_ref, v_ref, lw_ref, a_ref, gate_ref = refs[:6]
    nxt = 6
    if has_v:
        vf_ref, vg_ref = refs[6:8]
        nxt = 8
    kk_ref, ka_ref, rk_ref, gnw_ref, gnb_ref, o_ref, st_ref = refs[nxt:]

    @pl.when(pl.program_id(2) == 0)
    def _():
        st_ref[...] = jnp.zeros_like(st_ref)

    L = WKV_CHUNK
    n2 = 2 * L
    half = RWKV_HEAD
    m0 = lax.broadcasted_iota(jnp.int32, (L, LANES), 1) < half
    trow = lax.broadcasted_iota(jnp.int32, (L, LANES), 0)
    zeros = jnp.zeros((L, LANES), F32)

    def h0(x):
        return jnp.where(m0, x, 0.0)

    def h1(x):
        return jnp.where(m0, 0.0, x)

    def seg_sum(x):
        s0 = jnp.sum(h0(x), axis=-1, keepdims=True)
        s1 = jnp.sum(h1(x), axis=-1, keepdims=True)
        return jnp.where(m0, s0, s1)

    def vcat(*xs):
        return jnp.concatenate(xs, axis=0)

    row = lax.broadcasted_iota(jnp.int32, (n2, n2), 0)
    col = lax.broadcasted_iota(jnp.int32, (n2, n2), 1)
    keep = (col % L) < (row % L) + jnp.where(row < L, 0, 1)
    same_head = (row // L) == (col // L)
    steps = int(math.log2(L))

    def pair_stages(sub, p):
        sl = slice(p * LANES, (p + 1) * LANES)
        rs = slice(sub * L, (sub + 1) * L)
        r, k, v, lw, a = r_ref[rs, sl], k_ref[rs, sl], v_ref[rs, sl], lw_ref[rs, sl], a_ref[rs, sl]
        if has_v:
            v = v + (vf_ref[rs, sl] - v) * vg_ref[rs, sl]
        kq = k * kk_ref[:, sl]
        kk = kq * lax.rsqrt(jnp.maximum(seg_sum(kq * kq), 1e-24))
        k2 = k * (1.0 + (a - 1.0) * ka_ref[:, sl])
        b = kk * a
        c = lw
        step = 1
        while step < L:
            c = c + jnp.where(trow >= step, pltpu.roll(c, step, axis=0), 0.0)
            step *= 2
        c_last = c[L - 1:L, :]
        e_neg = jnp.exp(-c)
        e_last = jnp.exp(c_last - c)
        at, rt, bt, kt = -kk * jnp.exp(c - lw), r * jnp.exp(c), b * e_neg, k2 * e_neg
        vr = pltpu.roll(v, half, axis=1)
        bk = vcat(b * e_last, k2 * e_last)
        yield
        a0 = jnp.where(keep, _dot_nt(vcat(h0(at), h0(rt)), vcat(bt, kt)), 0.0)
        a1 = jnp.where(keep, _dot_nt(vcat(h1(at), h1(rt)), vcat(kt, bt)), 0.0)
        yield
        x = vcat(h0(at) + _dot(a0[:L], vcat(zeros, h1(vr))),
                 h1(at) + _dot(a1[:L], vcat(h0(vr), zeros)))
        pcat = jnp.where(m0, a0[:L], a1[:L])
        yield
        for s in range(steps):
            pbd = vcat(h0(pcat), h1(pcat))
            x = x + _dot(pbd, x)
            if s + 1 < steps:
                pcat = _dot(pcat, pbd)
            yield
        wcat = jnp.where(m0, x[:L], x[L:])
        u0 = pltpu.roll(jnp.where(m0, x[L:], x[:L]), half, axis=1)
        wh = _dot_nt(vcat(wcat, rt), st_ref[p])
        yield
        u = wh[:L] + u0
        y = wh[L:] + _dot(jnp.concatenate([a0[L:], a1[L:]], axis=1),
                          vcat(h0(u), h0(v), h1(v), h1(u)))
        st_ref[p] = st_ref[p] * jnp.exp(c_last) + jnp.where(
            same_head, _dot_tn(vcat(u, v), bk), 0.0)
        yield
        mean = seg_sum(y) * (1.0 / half)
        dlt = y - mean
        var = seg_sum(dlt * dlt) * (1.0 / half)
        yn = dlt * lax.rsqrt(var + GN_EPS) * gnw_ref[:, sl] + gnb_ref[:, sl]
        bonus = seg_sum(r * k2 * rk_ref[:, sl]) * v
        o_ref[rs, sl] = ((yn + bonus) * gate_ref[rs, sl]).astype(BF16)

    n_pairs = o_ref.shape[1] // LANES
    n_sub = o_ref.shape[0] // L
    plan = {}
    for sub in range(n_sub):
        t0 = sub * WKV_CHAIN
        for p in range(n_pairs):
            spread = 1 + (p * WKV_CHAIN) // n_pairs
            first = t0 if sub == 0 else t0 - WKV_CHAIN + spread
            last = t0 + WKV_CHAIN + (1 if sub == n_sub - 1 else spread)
            gen = pair_stages(sub, p)
            for order, tick in enumerate([first] + [t0 + s for s in range(1, WKV_CHAIN + 1)] + [last]):
                plan.setdefault(tick, []).append((order in (0, WKV_CHAIN + 1), sub, p, gen))
    for tick in sorted(plan):
        for _, _, _, gen in sorted(plan[tick], key=lambda e: e[:3]):
            next(gen, None)


def _wkv(rkv, lw, a, gate, v_first, v_gate, k_k, k_a, r_k, gn_w, gn_b, batch, width=2048):
    _, n, d = rkv.shape
    width = min(width, d)
    L = WKV_CHUNK * WKV_SUBCHUNKS
    nc = n // batch // L
    has_v = v_first is not None
    tile = pl.BlockSpec((L, width), lambda b, p, c: (b * nc + c, p))
    rkv_spec = lambda q: pl.BlockSpec((None, L, width), lambda b, p, c: (q, b * nc + c, p))
    prm = pl.BlockSpec((1, width), lambda b, p, c: (0, p))
    ins = [rkv, rkv, rkv, lw, a, gate]
    in_specs = [rkv_spec(0), rkv_spec(1), rkv_spec(2), tile, tile, tile]
    if has_v:
        ins += [v_first, v_gate]
        in_specs += [rkv_spec(2), tile]
    ins += [x.reshape(1, d) for x in (k_k, k_a, r_k, gn_w, gn_b)]
    in_specs += [prm] * 5
    return pl.pallas_call(
        functools.partial(_wkv_kernel, has_v=has_v),
        grid=(batch, d // width, nc), in_specs=in_specs, out_specs=tile,
        out_shape=jax.ShapeDtypeStruct((n, d), BF16),
        scratch_shapes=[pltpu.VMEM((width // LANES, LANES, LANES), F32)],
        compiler_params=_cparams(("parallel", "parallel", "arbitrary")))(*ins)


def _moba_kernel(q_ref, k_ref, v_ref, o_ref):
    bs = MOBA_BLOCK
    t = k_ref.shape[0]
    nb = t // bs
    nbp = 16
    qscale = (ATT_HEAD ** -0.5) * math.log2(math.e)
    kf = k_ref[...]
    kmean = jnp.mean(kf.reshape(nb, bs, ATT_HEAD), axis=1)
    kmean = jnp.concatenate([kmean, jnp.zeros((nbp - nb, ATT_HEAD), F32)], axis=0)
    km1, km2, _ = _split3(kmean)
    key_blk = lax.broadcasted_iota(jnp.int32, (t, LANES), 0) // bs
    key_lane = lax.broadcasted_iota(jnp.int32, (t, LANES), 1)
    k_aug = jnp.concatenate(
        [kf.astype(BF16), jnp.where(key_blk == key_lane, MASKED_LOGIT, 0.0).astype(BF16)], axis=1)
    v_aug = jnp.concatenate(
        [v_ref[...].astype(BF16), jnp.where(key_lane == 0, 1.0, 0.0).astype(BF16)], axis=1)
    eye = (lax.broadcasted_iota(jnp.int32, (nbp, LANES), 0)
           == lax.broadcasted_iota(jnp.int32, (nbp, LANES), 1)).astype(BF16)
    group = q_ref.shape[1] // ATT_HEAD
    rows = group * bs
    blk = lax.broadcasted_iota(jnp.int32, (nbp, rows), 0)
    qi = lax.broadcasted_iota(jnp.int32, (rows, bs), 0) % bs
    ki = lax.broadcasted_iota(jnp.int32, (rows, bs), 1)
    causal_bias = jnp.where(ki <= qi, 0.0, MASKED_LOGIT)
    for qb in range(nb):
        own = slice(qb * bs, (qb + 1) * bs)
        q = jnp.concatenate([q_ref[own, g * ATT_HEAD:(g + 1) * ATT_HEAD] for g in range(group)],
                            axis=0)
        qs = (q * qscale).astype(BF16)
        l_own = _dot_nt(qs, k_aug[own, :ATT_HEAD]) + causal_bias
        m = jnp.max(l_own, axis=-1, keepdims=True)
        if qb > 0:
            if qb > MOBA_TOPK:
                q1, q2, _ = _split3(q)
                gate = _dot_nt(km1, q1) + _dot_nt(km1, q2) + _dot_nt(km2, q1)
                cnt = jnp.zeros((nbp, rows), jnp.int32)
                for mb in range(qb):
                    gm = gate[mb:mb + 1, :]
                    beats = (gm > gate) | ((gm == gate) & (mb < blk))
                    cnt = cnt + beats.astype(jnp.int32)
                notsel = jnp.where((cnt >= MOBA_TOPK) & (blk < qb), 1.0, 0.0)
                flags = _dot_tn(notsel, eye).astype(BF16)
            else:
                flags = jnp.zeros((rows, LANES), BF16)
            l_past = _dot_nt(jnp.concatenate([qs, flags], axis=1), k_aug[:qb * bs])
            m = jnp.maximum(m, jnp.max(l_past, axis=-1, keepdims=True))
            acc = (_dot(jnp.exp2(l_past - m), v_aug[:qb * bs]) + _dot(jnp.exp2(l_own - m), v_aug[own]))
        else:
            acc = _dot(jnp.exp2(l_own - m), v_aug[own])
        out = (acc[:, :ATT_HEAD] / acc[:, ATT_HEAD:ATT_HEAD + 1]).astype(BF16)
        for g in range(group):
            o_ref[own, g * ATT_HEAD:(g + 1) * ATT_HEAD] = out[g * bs:(g + 1) * bs]


def _moba(q, kv, batch):
    n, dq = q.shape
    t = n // batch
    gw = dq // KV_HEADS
    return pl.pallas_call(
        _moba_kernel, grid=(batch, KV_HEADS),
        in_specs=[pl.BlockSpec((t, gw), lambda b, kh: (b, kh)),
                  pl.BlockSpec((t, ATT_HEAD), lambda b, kh: (b, kh)),
                  pl.BlockSpec((t, ATT_HEAD), lambda b, kh: (b, KV_HEADS + kh))],
        out_specs=pl.BlockSpec((t, gw), lambda b, kh: (b, kh)),
        out_shape=jax.ShapeDtypeStruct((n, dq), BF16),
        compiler_params=_cparams(("parallel", "parallel")))(q, kv, kv)


def _pad_lora(w_in, w_out):
    r = w_in.shape[1]
    rp = -(-r // LANES) * LANES
    return (jnp.pad(w_in, ((0, 0), (0, rp - r))).astype(BF16),
            jnp.pad(w_out, ((0, rp - r), (0, 0))).astype(BF16))


def _rope_tables(t):
    half = ATT_HEAD // 2
    inv = ROPE_THETA ** (-jnp.arange(half, dtype=F32) / half)
    ang = jnp.arange(t, dtype=F32)[:, None] * inv[None, :]
    cos, sin = jnp.cos(ang), jnp.sin(ang)
    return jnp.concatenate([cos, cos], axis=-1), jnp.concatenate([-sin, sin], axis=-1)


def kernel(x, ln_mix_g, ln_ffn_g, w_ff1, w_ff2, rw_mu, rw_w_rkv, rw_w0, rw_w1, rw_w2, rw_a0, rw_a1, rw_a2, rw_g1, rw_g2, rw_k_k, rw_k_a, rw_r_k, rw_gn_w, rw_gn_b, rw_w_o, rw_v0, rw_v1, rw_v2, kv_norm_g, w_kv, mb_w_q, mb_w_o, final_g):
    batch, t, d = x.shape
    n = batch * t
    depth = ln_mix_g.shape[0]
    n_rwkv = rw_mu.shape[0]
    h = x.reshape(n, d)
    cos, sin = _rope_tables(t)
    rkv_f32 = rw_w_rkv.reshape(n_rwkv, 3 * d, d)
    w1_b, w2_b, rkv_b, wo_b = (w[0:1].astype(BF16) for w in (w_ff1, w_ff2, rkv_f32, rw_w_o))
    v_first = None
    kv = None
    for layer in range(depth):
        if layer < n_rwkv:
            i = layer
            w1, w2 = _pad_lora(rw_w1[i], rw_w2[i])
            a1, a2 = _pad_lora(rw_a1[i], rw_a2[i])
            g1, g2 = _pad_lora(rw_g1[i], rw_g2[i])
            v_lora = None
            if i > 0:
                v1, v2 = _pad_lora(rw_v1[i - 1], rw_v2[i - 1])
                v_lora = (rw_v0[i - 1], v1, v2)
            prep = _rwkv_prep(h, t, ln_mix_g[layer], rw_mu[i], rw_w0[i], w1, w2,
                              rw_a0[i], a1, a2, g1, g2, v_lora)
            x3, lw, a, gate = prep[:4]
            v_gate = prep[4] if i > 0 else None
            rkv = _mm3(x3, rkv_b.reshape(1, 3, d, d), 0)
            yg = _wkv(rkv, lw, a, gate, v_first, v_gate, rw_k_k[i], rw_k_a[i], rw_r_k[i],
                      rw_gn_w[i], rw_gn_b[i], batch)
            if i == 0:
                v_first = rkv
            h = _mm_res(yg, wo_b, 0, h)
        else:
            if kv is None:
                kv = _norm_mm_rope(h, kv_norm_g, wkv_b, 0, cos, sin, KV_HEADS * ATT_HEAD, t)
            q = _norm_mm_rope(h, ln_mix_g[layer], wq_b, 0, cos, sin, mb_w_q.shape[2], t)
            att = _moba(q, kv, batch)
            h = _mm_res(att, wo_b, 0, h)
        nxt = layer + 1
        jobs = []
        if nxt < depth:
            jobs = [(w_ff1, nxt), (w_ff2, nxt)]
            if nxt < n_rwkv:
                jobs += [(rkv_f32, nxt), (rw_w_o, nxt)]
            else:
                jobs += [(mb_w_q, nxt - n_rwkv), (mb_w_o, nxt - n_rwkv)]
                if nxt == n_rwkv:
                    jobs.append((w_kv[None], 0))
        res = _mlp(h, ln_ffn_g[layer], w1_b, w2_b, 0, final_g, nxt == depth, jobs)
        h = res[0]
        if jobs:
            w1_b, w2_b = res[1:3]
            if nxt < n_rwkv:
                rkv_b, wo_b = res[3:5]
            else:
                wq_b, wo_b = res[3:5]
                if nxt == n_rwkv:
                    wkv_b = res[5]
    return h.reshape(batch, t, d)
```

```python
import functools
import math

import jax
import jax.numpy as jnp
from jax import lax
from jax.experimental import pallas as pl
from jax.experimental.pallas import tpu as pltpu

F32 = jnp.float32
BF16 = jnp.bfloat16

RMS_EPS = 1e-6
GN_EPS = 64e-5
RWKV_HEAD = 64
ATT_HEAD = 128
KV_HEADS = 4
MOBA_BLOCK = 256
MOBA_TOPK = 3
ROPE_THETA = 10000.0
LANES = 128
WKV_CHUNK = 64
WKV_CHAIN = 10
WKV_SUBCHUNKS = 2
MASKED_LOGIT = -1e30
VMEM_LIMIT = 56 * 1024 * 1024


def _cparams(sem):
    return pltpu.CompilerParams(dimension_semantics=sem, vmem_limit_bytes=VMEM_LIMIT)


def _rms(x, g):
    return x * lax.rsqrt(jnp.mean(x * x, axis=-1, keepdims=True) + RMS_EPS) * g


def _sigmoid(x):
    return 0.5 * jnp.tanh(0.5 * x) + 0.5


def _dot(a, b):
    return jnp.dot(a.astype(BF16), b.astype(BF16), preferred_element_type=F32)


def _dot_nt(a, b):
    return lax.dot_general(a.astype(BF16), b.astype(BF16), (((1,), (1,)), ((), ())),
                           preferred_element_type=F32)


def _dot_tn(a, b):
    return lax.dot_general(a.astype(BF16), b.astype(BF16), (((0,), (0,)), ((), ())),
                           preferred_element_type=F32)


def _split3(x):
    h1 = x.astype(BF16)
    r1 = x - h1.astype(F32)
    h2 = r1.astype(BF16)
    h3 = (r1 - h2.astype(F32)).astype(BF16)
    return h1, h2, h3


def _cast_tile(rows, cols, gi, gj):
    if (rows // gi) % 16 == 0 and (cols // gj) % LANES == 0:
        return (rows // gi, cols // gj), lambda i, j: (i, j)
    assert (rows // gj) % 16 == 0 and (cols // gi) % LANES == 0, (rows, cols, gi, gj)
    return (rows // gj, cols // gi), lambda i, j: (j, i)


def _cast_specs(cast_jobs, gi, gj, pick):
    ins, in_specs, out_shape, out_specs = [], [], [], []
    for w, idx in cast_jobs:
        _, rows, cols = w.shape
        blk, tile_index = _cast_tile(rows, cols, gi, gj)
        ins.append(w)
        in_specs.append(pl.BlockSpec(
            (None,) + blk, lambda *g, idx=idx, t=tile_index: (idx,) + t(*pick(*g))))
        out_shape.append(jax.ShapeDtypeStruct((1, rows, cols), BF16))
        out_specs.append(pl.BlockSpec((None,) + blk, lambda *g, t=tile_index: (0,) + t(*pick(*g))))
    return ins, in_specs, out_shape, out_specs


def _run_casts(src_refs, dst_refs):
    for src, dst in zip(src_refs, dst_refs):
        dst[...] = src[...].astype(BF16)


def _rwkv_prep_kernel(*refs, tiles_per_seq, has_v, n_cast):
    (h_ref, hprev_ref, g_ref, mu_ref, w0_ref, w1_ref, w2_ref, a0_ref, a1_ref, a2_ref,
     g1_ref, g2_ref) = refs[:12]
    n_in = 15 if has_v else 12
    n_out = 5 if has_v else 4
    if has_v:
        v0_ref, v1_ref, v2_ref = refs[12:15]
    outs = refs[n_in + n_cast:n_in + n_cast + n_out]
    x3_ref, lw_ref, a_ref, gate_ref = outs[:4]
    _run_casts(refs[n_in:n_in + n_cast], refs[n_in + n_cast + n_out:])
    i = pl.program_id(0)
    g = g_ref[...]
    hn = _rms(h_ref[...], g)
    prev = _rms(hprev_ref[7:8, :], g)
    prev = jnp.where(i % tiles_per_seq == 0, 0.0, prev)
    rows = lax.broadcasted_iota(jnp.int32, hn.shape, 0)
    shifted = jnp.where(rows == 0, prev, pltpu.roll(hn, 1, axis=0))
    xx = shifted - hn
    x3_ref[0] = (hn + xx * mu_ref[0:1, :]).astype(BF16)
    x3_ref[1] = (hn + xx * mu_ref[2:3, :]).astype(BF16)
    xv = (hn + xx * mu_ref[3:4, :]).astype(BF16)
    x3_ref[2] = xv
    xw = hn + xx * mu_ref[1:2, :]
    z = w0_ref[...] + _dot(jnp.tanh(_dot(xw, w1_ref[...])), w2_ref[...])
    lw_ref[...] = -math.exp(-0.5) * _sigmoid(z)
    xa = hn + xx * mu_ref[4:5, :]
    a_ref[...] = _sigmoid(a0_ref[...] + _dot(_dot(xa, a1_ref[...]), a2_ref[...])).astype(BF16)
    xg = hn + xx * mu_ref[5:6, :]
    gate_ref[...] = _dot(_sigmoid(_dot(xg, g1_ref[...])), g2_ref[...]).astype(BF16)
    if has_v:
        outs[4][...] = _sigmoid(
            v0_ref[...] + _dot(_dot(xv, v1_ref[...]), v2_ref[...])).astype(BF16)


def _rwkv_prep(h, seq_len, g, mu, w0, w1, w2, a0, a1, a2, g1, g2, v_lora, cast_jobs=(), tm=256):
    n, d = h.shape
    has_v = v_lora is not None
    row = lambda v: v.reshape(1, d)
    full = lambda arr: pl.BlockSpec(arr.shape, lambda i: (0,) * arr.ndim)
    tile = pl.BlockSpec((tm, d), lambda i: (i, 0))
    ins = [h, h, row(g), mu, row(w0), w1, w2, row(a0), a1, a2, g1, g2]
    in_specs = [tile, pl.BlockSpec((8, d), lambda i: (jnp.maximum(i * (tm // 8) - 1, 0), 0))]
    in_specs += [full(x) for x in ins[2:]]
    out_shape = [jax.ShapeDtypeStruct((3, n, d), BF16), jax.ShapeDtypeStruct((n, d), F32),
                 jax.ShapeDtypeStruct((n, d), BF16), jax.ShapeDtypeStruct((n, d), BF16)]
    out_specs = [pl.BlockSpec((3, tm, d), lambda i: (0, i, 0)), tile, tile, tile]
    if has_v:
        v0, v1, v2 = v_lora
        extra = [row(v0), v1, v2]
        ins += extra
        in_specs += [full(x) for x in extra]
        out_shape.append(jax.ShapeDtypeStruct((n, d), BF16))
        out_specs.append(tile)
    c_ins, c_in_specs, c_out_shape, c_out_specs = _cast_specs(
        cast_jobs, n // tm, 1, lambda i: (i, 0))
    ins, in_specs = ins + c_ins, in_specs + c_in_specs
    out_shape, out_specs = out_shape + c_out_shape, out_specs + c_out_specs
    return pl.pallas_call(
        functools.partial(_rwkv_prep_kernel, tiles_per_seq=seq_len // tm, has_v=has_v,
                          n_cast=len(cast_jobs)),
        grid=(n // tm,), in_specs=in_specs, out_specs=out_specs, out_shape=out_shape,
        compiler_params=_cparams(("parallel",)))(*ins)


def _mm3_kernel(x_ref, w_ref, o_ref):
    o_ref[...] = jnp.dot(x_ref[...], w_ref[...], preferred_element_type=F32)


def _mm3(x3, w4, layer, tm=1024, tn=1024):
    p, n, k = x3.shape
    m = w4.shape[3]
    tm, tn = min(tm, n), min(tn, m)
    return pl.pallas_call(
        _mm3_kernel, grid=(p, n // tm, m // tn),
        in_specs=[pl.BlockSpec((None, tm, k), lambda q, i, j: (q, i, 0)),
                  pl.BlockSpec((None, None, k, tn), lambda q, i, j: (layer, q, 0, j))],
        out_specs=pl.BlockSpec((None, tm, tn), lambda q, i, j: (q, i, j)),
        out_shape=jax.ShapeDtypeStruct((p, n, m), F32),
        compiler_params=_cparams(("parallel", "parallel", "arbitrary")))(x3, w4)


def _mm_res_kernel(x_ref, w_ref, r_ref, o_ref):
    o_ref[...] = r_ref[...] + jnp.dot(x_ref[...], w_ref[...], preferred_element_type=F32)


def _mm_res(x, w3, layer, res, tm=1024, tn=1024):
    n, k = x.shape
    m = w3.shape[2]
    tm, tn = min(tm, n), min(tn, m)
    return pl.pallas_call(
        _mm_res_kernel, grid=(n // tm, m // tn),
        in_specs=[pl.BlockSpec((tm, k), lambda i, j: (i, 0)),
                  pl.BlockSpec((None, k, tn), lambda i, j: (layer, 0, j)),
                  pl.BlockSpec((tm, tn), lambda i, j: (i, j))],
        out_specs=pl.BlockSpec((tm, tn), lambda i, j: (i, j)),
        out_shape=jax.ShapeDtypeStruct((n, m), F32),
        compiler_params=_cparams(("parallel", "arbitrary")))(x, w3, res)


def _norm_mm_rope_kernel(h_ref, g_ref, w_ref, cos_ref, sin_ref, o_ref, hn_ref, *, rope_tiles):
    j = pl.program_id(1)

    @pl.when(j == 0)
    def _():
        hn_ref[...] = _rms(h_ref[...], g_ref[...]).astype(BF16)

    acc = jnp.dot(hn_ref[...], w_ref[...], preferred_element_type=F32)

    @pl.when(j < rope_tiles)
    def _():
        cos, sin = cos_ref[...], sin_ref[...]
        for hh in range(acc.shape[1] // ATT_HEAD):
            x = acc[:, hh * ATT_HEAD:(hh + 1) * ATT_HEAD]
            o_ref[:, hh * ATT_HEAD:(hh + 1) * ATT_HEAD] = (
                x * cos + pltpu.roll(x, ATT_HEAD // 2, axis=1) * sin)

    @pl.when(j >= rope_tiles)
    def _():
        o_ref[...] = acc


def _norm_mm_rope(h, g, w3, layer, cos, sin, rope_cols, seq_len, tm=1024, tn=1024):
    n, d = h.shape
    m = w3.shape[2]
    tm, tn = min(tm, seq_len), min(tn, rope_cols)
    return pl.pallas_call(
        functools.partial(_norm_mm_rope_kernel, rope_tiles=rope_cols // tn),
        grid=(n // tm, m // tn),
        in_specs=[pl.BlockSpec((tm, d), lambda i, j: (i, 0)),
                  pl.BlockSpec((1, d), lambda i, j: (0, 0)),
                  pl.BlockSpec((None, d, tn), lambda i, j: (layer, 0, j)),
                  pl.BlockSpec((tm, ATT_HEAD), lambda i, j: (i % (seq_len // tm), 0)),
                  pl.BlockSpec((tm, ATT_HEAD), lambda i, j: (i % (seq_len // tm), 0))],
        out_specs=pl.BlockSpec((tm, tn), lambda i, j: (i, j)),
        out_shape=jax.ShapeDtypeStruct((n, m), F32),
        scratch_shapes=[pltpu.VMEM((tm, d), BF16)],
        compiler_params=_cparams(("parallel", "arbitrary")))(h, g.reshape(1, d), w3, cos, sin)


def _mlp_kernel(*refs, final_norm, n_cast):
    h_ref, g_ref, w1_ref, w2_ref, gf_ref = refs[:5]
    src_refs = refs[5:5 + n_cast]
    o_ref = refs[5 + n_cast]
    dst_refs = refs[6 + n_cast:6 + 2 * n_cast]
    hn_ref = refs[6 + 2 * n_cast]
    _run_casts(src_refs, dst_refs)
    j = pl.program_id(1)

    @pl.when(j == 0)
    def _():
        h = h_ref[...]
        hn_ref[...] = _rms(h, g_ref[...]).astype(BF16)
        o_ref[...] = h

    u = jnp.dot(hn_ref[...], w1_ref[...], preferred_element_type=F32)
    u = jnp.square(jnp.maximum(u, 0.0)).astype(BF16)
    o_ref[...] += jnp.dot(u, w2_ref[...], preferred_element_type=F32)

    if final_norm:
        @pl.when(j == pl.num_programs(1) - 1)
        def _():
            o_ref[...] = _rms(o_ref[...], gf_ref[...])


def _mlp(h, g, w1, w2, layer, gf, final_norm, cast_jobs=(), tm=1024, tf=512):
    n, d = h.shape
    f = w1.shape[2]
    tm = min(tm, n)
    gi, gj = n // tm, f // tf
    ins = [h, g.reshape(1, d), w1, w2, gf.reshape(1, d)]
    in_specs = [pl.BlockSpec((tm, d), lambda i, j: (i, 0)),
                pl.BlockSpec((1, d), lambda i, j: (0, 0)),
                pl.BlockSpec((None, d, tf), lambda i, j: (layer, 0, j)),
                pl.BlockSpec((None, tf, d), lambda i, j: (layer, j, 0)),
                pl.BlockSpec((1, d), lambda i, j: (0, 0))]
    out_shape = [jax.ShapeDtypeStruct((n, d), F32)]
    out_specs = [pl.BlockSpec((tm, d), lambda i, j: (i, 0))]
    c_ins, c_in_specs, c_out_shape, c_out_specs = _cast_specs(
        cast_jobs, gi, gj, lambda i, j: (i, j))
    ins, in_specs = ins + c_ins, in_specs + c_in_specs
    out_shape, out_specs = out_shape + c_out_shape, out_specs + c_out_specs
    return pl.pallas_call(
        functools.partial(_mlp_kernel, final_norm=final_norm, n_cast=len(cast_jobs)),
        grid=(gi, gj), in_specs=in_specs, out_specs=out_specs, out_shape=out_shape,
        scratch_shapes=[pltpu.VMEM((tm, d), BF16)],
        compiler_params=_cparams(("parallel", "arbitrary")))(*ins)


def _wkv_kernel(*refs, has_v, n_cast):
    r_ref, k_ref, v_ref, lw_ref, a_ref, gate_ref = refs[:6]
    nxt = 6
    if has_v:
        vf_ref, vg_ref = refs[6:8]
        nxt = 8
    kk_ref, ka_ref, rk_ref, gnw_ref, gnb_ref = refs[nxt:nxt + 5]
    o_ref = refs[nxt + 5 + n_cast]
    st_ref = refs[-1]
    _run_casts(refs[nxt + 5:nxt + 5 + n_cast], refs[nxt + 6 + n_cast:-1])

    @pl.when(pl.program_id(2) == 0)
    def _():
        st_ref[...] = jnp.zeros_like(st_ref)

    L = WKV_CHUNK
    n2 = 2 * L
    half = RWKV_HEAD
    m0 = lax.broadcasted_iota(jnp.int32, (L, LANES), 1) < half
    zeros = jnp.zeros((L, LANES), F32)
    tri = (lax.broadcasted_iota(jnp.int32, (L, L), 1)
           <= lax.broadcasted_iota(jnp.int32, (L, L), 0)).astype(BF16)

    def h0(x):
        return jnp.where(m0, x, 0.0)

    def h1(x):
        return jnp.where(m0, 0.0, x)

    def seg_sum(x):
        s0 = jnp.sum(h0(x), axis=-1, keepdims=True)
        s1 = jnp.sum(h1(x), axis=-1, keepdims=True)
        return jnp.where(m0, s0, s1)

    def vcat(*xs):
        return jnp.concatenate(xs, axis=0)

    row = lax.broadcasted_iota(jnp.int32, (n2, n2), 0)
    col = lax.broadcasted_iota(jnp.int32, (n2, n2), 1)
    keep = (col % L) < (row % L) + jnp.where(row < L, 0, 1)
    same_head = (row // L) == (col // L)
    steps = int(math.log2(L))

    def pair_stages(sub, p):
        sl = slice(p * LANES, (p + 1) * LANES)
        rs = slice(sub * L, (sub + 1) * L)
        r, k, v, lw = r_ref[rs, sl], k_ref[rs, sl], v_ref[rs, sl], lw_ref[rs, sl]
        a = a_ref[rs, sl].astype(F32)
        if has_v:
            v = v + (vf_ref[rs, sl] - v) * vg_ref[rs, sl].astype(F32)
        kq = k * kk_ref[:, sl]
        kk = kq * lax.rsqrt(jnp.maximum(seg_sum(kq * kq), 1e-24))
        k2 = k * (1.0 + (a - 1.0) * ka_ref[:, sl])
        b = kk * a
        lw_hi = lw.astype(BF16)
        lw_lo = (lw - lw_hi.astype(F32)).astype(BF16)
        c = (jnp.dot(tri, lw_hi, preferred_element_type=F32)
             + jnp.dot(tri, lw_lo, preferred_element_type=F32))
        c_last = c[L - 1:L, :]
        e_neg = jnp.exp(-c)
        e_last = jnp.exp(c_last - c)
        at, rt, bt, kt = -kk * jnp.exp(c - lw), r * jnp.exp(c), b * e_neg, k2 * e_neg
        vr = pltpu.roll(v, half, axis=1)
        bk = vcat(b * e_last, k2 * e_last)
        yield
        a0 = jnp.where(keep, _dot_nt(vcat(h0(at), h0(rt)), vcat(bt, kt)), 0.0)
        a1 = jnp.where(keep, _dot_nt(vcat(h1(at), h1(rt)), vcat(kt, bt)), 0.0)
        yield
        x = vcat(h0(at) + _dot(a0[:L], vcat(zeros, h1(vr))),
                 h1(at) + _dot(a1[:L], vcat(h0(vr), zeros)))
        pcat = jnp.where(m0, a0[:L], a1[:L])
        yield
        for s in range(steps):
            pbd = vcat(h0(pcat), h1(pcat))
            x = x + _dot(pbd, x)
            if s + 1 < steps:
                pcat = _dot(pcat, pbd)
            yield
        wcat = jnp.where(m0, x[:L], x[L:])
        u0 = pltpu.roll(jnp.where(m0, x[L:], x[:L]), half, axis=1)
        wh = _dot_nt(vcat(wcat, rt), st_ref[p])
        yield
        u = wh[:L] + u0
        y = wh[L:] + _dot(jnp.concatenate([a0[L:], a1[L:]], axis=1),
                          vcat(h0(u), h0(v), h1(v), h1(u)))
        st_ref[p] = st_ref[p] * jnp.exp(c_last) + jnp.where(
            same_head, _dot_tn(vcat(u, v), bk), 0.0)
        yield
        mean = seg_sum(y) * (1.0 / half)
        dlt = y - mean
        var = seg_sum(dlt * dlt) * (1.0 / half)
        yn = dlt * lax.rsqrt(var + GN_EPS) * gnw_ref[:, sl] + gnb_ref[:, sl]
        bonus = seg_sum(r * k2 * rk_ref[:, sl]) * v
        o_ref[rs, sl] = ((yn + bonus) * gate_ref[rs, sl].astype(F32)).astype(BF16)

    n_pairs = o_ref.shape[1] // LANES
    n_sub = o_ref.shape[0] // L
    plan = {}
    for sub in range(n_sub):
        t0 = sub * WKV_CHAIN
        for p in range(n_pairs):
            spread = 1 + (p * WKV_CHAIN) // n_pairs
            first = t0 if sub == 0 else t0 - WKV_CHAIN + spread
            last = t0 + WKV_CHAIN + (1 if sub == n_sub - 1 else spread)
            gen = pair_stages(sub, p)
            for order, tick in enumerate([first] + [t0 + s for s in range(1, WKV_CHAIN + 1)] + [last]):
                plan.setdefault(tick, []).append((order in (0, WKV_CHAIN + 1), sub, p, gen))
    for tick in sorted(plan):
        for _, _, _, gen in sorted(plan[tick], key=lambda e: e[:3]):
            next(gen, None)


def _wkv(rkv, lw, a, gate, v_first, v_gate, k_k, k_a, r_k, gn_w, gn_b, batch, cast_jobs=(),
         width=2048):
    _, n, d = rkv.shape
    width = min(width, d)
    assert not cast_jobs or width == d
    L = WKV_CHUNK * WKV_SUBCHUNKS
    nc = n // batch // L
    has_v = v_first is not None
    tile = pl.BlockSpec((L, width), lambda b, p, c: (b * nc + c, p))
    rkv_spec = lambda q: pl.BlockSpec((None, L, width), lambda b, p, c: (q, b * nc + c, p))
    prm = pl.BlockSpec((1, width), lambda b, p, c: (0, p))
    ins = [rkv, rkv, rkv, lw, a, gate]
    in_specs = [rkv_spec(0), rkv_spec(1), rkv_spec(2), tile, tile, tile]
    if has_v:
        ins += [v_first, v_gate]
        in_specs += [rkv_spec(2), tile]
    ins += [x.reshape(1, d) for x in (k_k, k_a, r_k, gn_w, gn_b)]
    in_specs += [prm] * 5
    c_ins, c_in_specs, c_out_shape, c_out_specs = _cast_specs(
        cast_jobs, batch, nc, lambda b, p, c: (b, c))
    return pl.pallas_call(
        functools.partial(_wkv_kernel, has_v=has_v, n_cast=len(cast_jobs)),
        grid=(batch, d // width, nc), in_specs=in_specs + c_in_specs,
        out_specs=[tile] + c_out_specs,
        out_shape=[jax.ShapeDtypeStruct((n, d), BF16)] + c_out_shape,
        scratch_shapes=[pltpu.VMEM((width // LANES, LANES, LANES), F32)],
        compiler_params=_cparams(("parallel", "parallel", "arbitrary")))(*ins, *c_ins)


def _moba_kernel(q_ref, k_ref, v_ref, o_ref):
    bs = MOBA_BLOCK
    t = k_ref.shape[0]
    nb = t // bs
    nbp = 16
    qscale = (ATT_HEAD ** -0.5) * math.log2(math.e)
    kf = k_ref[...]
    kmean = jnp.mean(kf.reshape(nb, bs, ATT_HEAD), axis=1)
    kmean = jnp.concatenate([kmean, jnp.zeros((nbp - nb, ATT_HEAD), F32)], axis=0)
    km1, km2, _ = _split3(kmean)
    key_blk = lax.broadcasted_iota(jnp.int32, (t, LANES), 0) // bs
    key_lane = lax.broadcasted_iota(jnp.int32, (t, LANES), 1)
    k_aug = jnp.concatenate(
        [kf.astype(BF16), jnp.where(key_blk == key_lane, MASKED_LOGIT, 0.0).astype(BF16)], axis=1)
    v_aug = jnp.concatenate(
        [v_ref[...].astype(BF16), jnp.where(key_lane == 0, 1.0, 0.0).astype(BF16)], axis=1)
    eye = (lax.broadcasted_iota(jnp.int32, (nbp, LANES), 0)
           == lax.broadcasted_iota(jnp.int32, (nbp, LANES), 1)).astype(BF16)
    group = q_ref.shape[1] // ATT_HEAD
    rows = group * bs
    blk = lax.broadcasted_iota(jnp.int32, (nbp, rows), 0)
    qi = lax.broadcasted_iota(jnp.int32, (rows, bs), 0) % bs
    ki = lax.broadcasted_iota(jnp.int32, (rows, bs), 1)
    causal_bias = jnp.where(ki <= qi, 0.0, MASKED_LOGIT)
    for qb in range(nb):
        own = slice(qb * bs, (qb + 1) * bs)
        q = jnp.concatenate([q_ref[own, g * ATT_HEAD:(g + 1) * ATT_HEAD] for g in range(group)],
                            axis=0)
        qs = (q * qscale).astype(BF16)
        l_own = _dot_nt(qs, k_aug[own, :ATT_HEAD]) + causal_bias
        m = jnp.max(l_own, axis=-1, keepdims=True)
        if qb > 0:
            if qb > MOBA_TOPK:
                q1, q2, _ = _split3(q)
                gate = _dot_nt(km1, q1) + _dot_nt(km1, q2) + _dot_nt(km2, q1)
                cnt = jnp.zeros((nbp, rows), jnp.int32)
                for mb in range(qb):
                    gm = gate[mb:mb + 1, :]
                    beats = (gm > gate) | ((gm == gate) & (mb < blk))
                    cnt = cnt + beats.astype(jnp.int32)
                notsel = jnp.where((cnt >= MOBA_TOPK) & (blk < qb), 1.0, 0.0)
                flags = _dot_tn(notsel, eye).astype(BF16)
            else:
                flags = jnp.zeros((rows, LANES), BF16)
            l_past = _dot_nt(jnp.concatenate([qs, flags], axis=1), k_aug[:qb * bs])
            m = jnp.maximum(m, jnp.max(l_past, axis=-1, keepdims=True))
            acc = (_dot(jnp.exp2(l_past - m), v_aug[:qb * bs]) + _dot(jnp.exp2(l_own - m), v_aug[own]))
        else:
            acc = _dot(jnp.exp2(l_own - m), v_aug[own])
        out = (acc[:, :ATT_HEAD] / acc[:, ATT_HEAD:ATT_HEAD + 1]).astype(BF16)
        for g in range(group):
            o_ref[own, g * ATT_HEAD:(g + 1) * ATT_HEAD] = out[g * bs:(g + 1) * bs]


def _moba(q, kv, batch):
    n, dq = q.shape
    t = n // batch
    gw = dq // KV_HEADS
    return pl.pallas_call(
        _moba_kernel, grid=(batch, KV_HEADS),
        in_specs=[pl.BlockSpec((t, gw), lambda b, kh: (b, kh)),
                  pl.BlockSpec((t, ATT_HEAD), lambda b, kh: (b, kh)),
                  pl.BlockSpec((t, ATT_HEAD), lambda b, kh: (b, KV_HEADS + kh))],
        out_specs=pl.BlockSpec((t, gw), lambda b, kh: (b, kh)),
        out_shape=jax.ShapeDtypeStruct((n, dq), BF16),
        compiler_params=_cparams(("parallel", "parallel")))(q, kv, kv)


def _pad_lora(w_in, w_out):
    r = w_in.shape[1]
    rp = -(-r // LANES) * LANES
    return (jnp.pad(w_in, ((0, 0), (0, rp - r))).astype(BF16),
            jnp.pad(w_out, ((0, rp - r), (0, 0))).astype(BF16))


def _rope_tables(t):
    half = ATT_HEAD // 2
    inv = ROPE_THETA ** (-jnp.arange(half, dtype=F32) / half)
    ang = jnp.arange(t, dtype=F32)[:, None] * inv[None, :]
    cos, sin = jnp.cos(ang), jnp.sin(ang)
    return jnp.concatenate([cos, cos], axis=-1), jnp.concatenate([-sin, sin], axis=-1)


def kernel(x, ln_mix_g, ln_ffn_g, w_ff1, w_ff2, rw_mu, rw_w_rkv, rw_w0, rw_w1, rw_w2, rw_a0, rw_a1, rw_a2, rw_g1, rw_g2, rw_k_k, rw_k_a, rw_r_k, rw_gn_w, rw_gn_b, rw_w_o, rw_v0, rw_v1, rw_v2, kv_norm_g, w_kv, mb_w_q, mb_w_o, final_g):
    batch, t, d = x.shape
    n = batch * t
    depth = ln_mix_g.shape[0]
    n_rwkv = rw_mu.shape[0]
    h = x.reshape(n, d)
    cos, sin = _rope_tables(t)
    rkv_f32 = rw_w_rkv.reshape(n_rwkv, 3 * d, d)
    v_first = None
    kv = None
    for layer in range(depth):
        if layer < n_rwkv:
            i = layer
            w1, w2 = _pad_lora(rw_w1[i], rw_w2[i])
            a1, a2 = _pad_lora(rw_a1[i], rw_a2[i])
            g1, g2 = _pad_lora(rw_g1[i], rw_g2[i])
            v_lora = None
            if i > 0:
                v1, v2 = _pad_lora(rw_v1[i - 1], rw_v2[i - 1])
                v_lora = (rw_v0[i - 1], v1, v2)
            first = layer == 0
            prep = _rwkv_prep(h, t, ln_mix_g[layer], rw_mu[i], rw_w0[i], w1, w2,
                              rw_a0[i], a1, a2, g1, g2, v_lora,
                              [(rkv_f32, 0)] if first else [])
            x3, lw, a, gate = prep[:4]
            v_gate = prep[4] if i > 0 else None
            if first:
                rkv_b = prep[-1]
            rkv = _mm3(x3, rkv_b.reshape(1, 3, d, d), 0)
            res = _wkv(rkv, lw, a, gate, v_first, v_gate, rw_k_k[i], rw_k_a[i], rw_r_k[i],
                       rw_gn_w[i], rw_gn_b[i], batch,
                       [(w_ff1, 0), (w_ff2, 0), (rw_w_o, 0)] if first else [])
            yg = res[0]
            if first:
                w1_b, w2_b, wo_b = res[1:]
            if i == 0:
                v_first = rkv
            h = _mm_res(yg, wo_b, 0, h)
        else:
            if kv is None:
                kv = _norm_mm_rope(h, kv_norm_g, wkv_b, 0, cos, sin, KV_HEADS * ATT_HEAD, t)
            q = _norm_mm_rope(h, ln_mix_g[layer], wq_b, 0, cos, sin, mb_w_q.shape[2], t)
            att = _moba(q, kv, batch)
            h = _mm_res(att, wo_b, 0, h)
        nxt = layer + 1
        jobs = []
        if nxt < depth:
            jobs = [(w_ff1, nxt), (w_ff2, nxt)]
            if nxt < n_rwkv:
                jobs += [(rkv_f32, nxt), (rw_w_o, nxt)]
            else:
                jobs += [(mb_w_q, nxt - n_rwkv), (mb_w_o, nxt - n_rwkv)]
                if nxt == n_rwkv:
                    jobs.append((w_kv[None], 0))
        res = _mlp(h, ln_ffn_g[layer], w1_b, w2_b, 0, final_g, nxt == depth, jobs)
        h = res[0]
        if jobs:
            w1_b, w2_b = res[1:3]
            if nxt < n_rwkv:
                rkv_b, wo_b = res[3:5]
            else:
                wq_b, wo_b = res[3:5]
                if nxt == n_rwkv:
                    wkv_b = res[5]
    return h.reshape(batch, t, d)
```

```python
import functools
import math

import jax
import jax.numpy as jnp
from jax import lax
from jax.experimental import pallas as pl
from jax.experimental.pallas import tpu as pltpu

F32 = jnp.float32
BF16 = jnp.bfloat16

RMS_EPS = 1e-6
GN_EPS = 64e-5
RWKV_HEAD = 64
ATT_HEAD = 128
KV_HEADS = 4
MOBA_BLOCK = 256
MOBA_TOPK = 3
ROPE_THETA = 10000.0
LANES = 128
WKV_CHUNK = 64
WKV_CHAIN = 10
WKV_SUBCHUNKS = 4
MASKED_LOGIT = -1e30
VMEM_LIMIT = 56 * 1024 * 1024


def _cparams(sem):
    return pltpu.CompilerParams(dimension_semantics=sem, vmem_limit_bytes=VMEM_LIMIT)


def _rms(x, g):
    return x * lax.rsqrt(jnp.mean(x * x, axis=-1, keepdims=True) + RMS_EPS) * g


def _sigmoid(x):
    return 0.5 * jnp.tanh(0.5 * x) + 0.5


def _dot(a, b):
    return jnp.dot(a.astype(BF16), b.astype(BF16), preferred_element_type=F32)


def _dot_nt(a, b):
    return lax.dot_general(a.astype(BF16), b.astype(BF16), (((1,), (1,)), ((), ())),
                           preferred_element_type=F32)


def _dot_tn(a, b):
    return lax.dot_general(a.astype(BF16), b.astype(BF16), (((0,), (0,)), ((), ())),
                           preferred_element_type=F32)


def _split3(x):
    h1 = x.astype(BF16)
    r1 = x - h1.astype(F32)
    h2 = r1.astype(BF16)
    h3 = (r1 - h2.astype(F32)).astype(BF16)
    return h1, h2, h3


def _cast_tile(rows, cols, gi, gj):
    if (rows // gi) % 16 == 0 and (cols // gj) % LANES == 0:
        return (rows // gi, cols // gj), lambda i, j: (i, j)
    assert (rows // gj) % 16 == 0 and (cols // gi) % LANES == 0, (rows, cols, gi, gj)
    return (rows // gj, cols // gi), lambda i, j: (j, i)


def _cast_specs(cast_jobs, gi, gj, pick):
    ins, in_specs, out_shape, out_specs = [], [], [], []
    for w, idx in cast_jobs:
        _, rows, cols = w.shape
        blk, tile_index = _cast_tile(rows, cols, gi, gj)
        ins.append(w)
        in_specs.append(pl.BlockSpec(
            (None,) + blk, lambda *g, idx=idx, t=tile_index: (idx,) + t(*pick(*g))))
        out_shape.append(jax.ShapeDtypeStruct((1, rows, cols), BF16))
        out_specs.append(pl.BlockSpec((None,) + blk, lambda *g, t=tile_index: (0,) + t(*pick(*g))))
    return ins, in_specs, out_shape, out_specs


def _run_casts(src_refs, dst_refs):
    for src, dst in zip(src_refs, dst_refs):
        dst[...] = src[...].astype(BF16)


def _rwkv_prep_kernel(*refs, tiles_per_seq, has_v, n_cast):
    (h_ref, hprev_ref, g_ref, mu_ref, w0_ref, w1_ref, w2_ref, a0_ref, a1_ref, a2_ref,
     g1_ref, g2_ref) = refs[:12]
    n_in = 15 if has_v else 12
    n_out = 5 if has_v else 4
    if has_v:
        v0_ref, v1_ref, v2_ref = refs[12:15]
    outs = refs[n_in + n_cast:n_in + n_cast + n_out]
    x3_ref, lw_ref, a_ref, gate_ref = outs[:4]
    _run_casts(refs[n_in:n_in + n_cast], refs[n_in + n_cast + n_out:])
    i = pl.program_id(0)
    g = g_ref[...]
    hn = _rms(h_ref[...], g)
    prev = _rms(hprev_ref[7:8, :], g)
    prev = jnp.where(i % tiles_per_seq == 0, 0.0, prev)
    rows = lax.broadcasted_iota(jnp.int32, hn.shape, 0)
    shifted = jnp.where(rows == 0, prev, pltpu.roll(hn, 1, axis=0))
    xx = shifted - hn
    x3_ref[0] = (hn + xx * mu_ref[0:1, :]).astype(BF16)
    x3_ref[1] = (hn + xx * mu_ref[2:3, :]).astype(BF16)
    xv = (hn + xx * mu_ref[3:4, :]).astype(BF16)
    x3_ref[2] = xv
    xw = hn + xx * mu_ref[1:2, :]
    z = w0_ref[...] + _dot(jnp.tanh(_dot(xw, w1_ref[...])), w2_ref[...])
    lw_ref[...] = -math.exp(-0.5) * _sigmoid(z)
    xa = hn + xx * mu_ref[4:5, :]
    a_ref[...] = _sigmoid(a0_ref[...] + _dot(_dot(xa, a1_ref[...]), a2_ref[...])).astype(BF16)
    xg = hn + xx * mu_ref[5:6, :]
    gate_ref[...] = _dot(_sigmoid(_dot(xg, g1_ref[...])), g2_ref[...]).astype(BF16)
    if has_v:
        outs[4][...] = _sigmoid(
            v0_ref[...] + _dot(_dot(xv, v1_ref[...]), v2_ref[...])).astype(BF16)


def _rwkv_prep(h, seq_len, g, mu, w0, w1, w2, a0, a1, a2, g1, g2, v_lora, cast_jobs=(), tm=256):
    n, d = h.shape
    has_v = v_lora is not None
    row = lambda v: v.reshape(1, d)
    full = lambda arr: pl.BlockSpec(arr.shape, lambda i: (0,) * arr.ndim)
    tile = pl.BlockSpec((tm, d), lambda i: (i, 0))
    ins = [h, h, row(g), mu, row(w0), w1, w2, row(a0), a1, a2, g1, g2]
    in_specs = [tile, pl.BlockSpec((8, d), lambda i: (jnp.maximum(i * (tm // 8) - 1, 0), 0))]
    in_specs += [full(x) for x in ins[2:]]
    out_shape = [jax.ShapeDtypeStruct((3, n, d), BF16), jax.ShapeDtypeStruct((n, d), F32),
                 jax.ShapeDtypeStruct((n, d), BF16), jax.ShapeDtypeStruct((n, d), BF16)]
    out_specs = [pl.BlockSpec((3, tm, d), lambda i: (0, i, 0)), tile, tile, tile]
    if has_v:
        v0, v1, v2 = v_lora
        extra = [row(v0), v1, v2]
        ins += extra
        in_specs += [full(x) for x in extra]
        out_shape.append(jax.ShapeDtypeStruct((n, d), BF16))
        out_specs.append(tile)
    c_ins, c_in_specs, c_out_shape, c_out_specs = _cast_specs(
        cast_jobs, n // tm, 1, lambda i: (i, 0))
    ins, in_specs = ins + c_ins, in_specs + c_in_specs
    out_shape, out_specs = out_shape + c_out_shape, out_specs + c_out_specs
    return pl.pallas_call(
        functools.partial(_rwkv_prep_kernel, tiles_per_seq=seq_len // tm, has_v=has_v,
                          n_cast=len(cast_jobs)),
        grid=(n // tm,), in_specs=in_specs, out_specs=out_specs, out_shape=out_shape,
        compiler_params=_cparams(("parallel",)))(*ins)


def _mm3_kernel(x_ref, w_ref, o_ref):
    o_ref[...] = jnp.dot(x_ref[...], w_ref[...], preferred_element_type=F32)


def _mm3(x3, w4, layer, tm=1024, tn=2048):
    p, n, k = x3.shape
    m = w4.shape[3]
    tm, tn = min(tm, n), min(tn, m)
    return pl.pallas_call(
        _mm3_kernel, grid=(p, n // tm, m // tn),
        in_specs=[pl.BlockSpec((None, tm, k), lambda q, i, j: (q, i, 0)),
                  pl.BlockSpec((None, None, k, tn), lambda q, i, j: (layer, q, 0, j))],
        out_specs=pl.BlockSpec((None, tm, tn), lambda q, i, j: (q, i, j)),
        out_shape=jax.ShapeDtypeStruct((p, n, m), F32),
        compiler_params=_cparams(("parallel", "parallel", "arbitrary")))(x3, w4)


def _mm_res_kernel(x_ref, w_ref, r_ref, o_ref):
    o_ref[...] = r_ref[...] + jnp.dot(x_ref[...], w_ref[...], preferred_element_type=F32)


def _mm_res(x, w3, layer, res, tm=1024, tn=1024):
    n, k = x.shape
    m = w3.shape[2]
    tm, tn = min(tm, n), min(tn, m)
    return pl.pallas_call(
        _mm_res_kernel, grid=(n // tm, m // tn),
        in_specs=[pl.BlockSpec((tm, k), lambda i, j: (i, 0)),
                  pl.BlockSpec((None, k, tn), lambda i, j: (layer, 0, j)),
                  pl.BlockSpec((tm, tn), lambda i, j: (i, j))],
        out_specs=pl.BlockSpec((tm, tn), lambda i, j: (i, j)),
        out_shape=jax.ShapeDtypeStruct((n, m), F32),
        compiler_params=_cparams(("parallel", "arbitrary")))(x, w3, res)


def _norm_mm_rope_kernel(h_ref, g_ref, w_ref, cos_ref, sin_ref, o_ref, hn_ref, *, rope_tiles):
    j = pl.program_id(1)

    @pl.when(j == 0)
    def _():
        hn_ref[...] = _rms(h_ref[...], g_ref[...]).astype(BF16)

    acc = jnp.dot(hn_ref[...], w_ref[...], preferred_element_type=F32)

    @pl.when(j < rope_tiles)
    def _():
        cos, sin = cos_ref[...], sin_ref[...]
        for hh in range(acc.shape[1] // ATT_HEAD):
            x = acc[:, hh * ATT_HEAD:(hh + 1) * ATT_HEAD]
            o_ref[:, hh * ATT_HEAD:(hh + 1) * ATT_HEAD] = (
                x * cos + pltpu.roll(x, ATT_HEAD // 2, axis=1) * sin)

    @pl.when(j >= rope_tiles)
    def _():
        o_ref[...] = acc


def _norm_mm_rope(h, g, w3, layer, cos, sin, rope_cols, seq_len, tm=1024, tn=1024):
    n, d = h.shape
    m = w3.shape[2]
    tm, tn = min(tm, seq_len), min(tn, rope_cols)
    return pl.pallas_call(
        functools.partial(_norm_mm_rope_kernel, rope_tiles=rope_cols // tn),
        grid=(n // tm, m // tn),
        in_specs=[pl.BlockSpec((tm, d), lambda i, j: (i, 0)),
                  pl.BlockSpec((1, d), lambda i, j: (0, 0)),
                  pl.BlockSpec((None, d, tn), lambda i, j: (layer, 0, j)),
                  pl.BlockSpec((tm, ATT_HEAD), lambda i, j: (i % (seq_len // tm), 0)),
                  pl.BlockSpec((tm, ATT_HEAD), lambda i, j: (i % (seq_len // tm), 0))],
        out_specs=pl.BlockSpec((tm, tn), lambda i, j: (i, j)),
        out_shape=jax.ShapeDtypeStruct((n, m), F32),
        scratch_shapes=[pltpu.VMEM((tm, d), BF16)],
        compiler_params=_cparams(("parallel", "arbitrary")))(h, g.reshape(1, d), w3, cos, sin)


def _mlp_kernel(*refs, final_norm, n_cast):
    h_ref, g_ref, w1_ref, w2_ref, gf_ref = refs[:5]
    src_refs = refs[5:5 + n_cast]
    o_ref = refs[5 + n_cast]
    dst_refs = refs[6 + n_cast:6 + 2 * n_cast]
    hn_ref = refs[6 + 2 * n_cast]
    _run_casts(src_refs, dst_refs)
    j = pl.program_id(1)

    @pl.when(j == 0)
    def _():
        h = h_ref[...]
        hn_ref[...] = _rms(h, g_ref[...]).astype(BF16)
        o_ref[...] = h

    u = jnp.dot(hn_ref[...], w1_ref[...], preferred_element_type=F32)
    u = jnp.square(jnp.maximum(u, 0.0)).astype(BF16)
    o_ref[...] += jnp.dot(u, w2_ref[...], preferred_element_type=F32)

    if final_norm:
        @pl.when(j == pl.num_programs(1) - 1)
        def _():
            o_ref[...] = _rms(o_ref[...], gf_ref[...])


def _mlp(h, g, w1, w2, layer, gf, final_norm, cast_jobs=(), tm=1024, tf=512):
    n, d = h.shape
    f = w1.shape[2]
    tm = min(tm, n)
    gi, gj = n // tm, f // tf
    ins = [h, g.reshape(1, d), w1, w2, gf.reshape(1, d)]
    in_specs = [pl.BlockSpec((tm, d), lambda i, j: (i, 0)),
                pl.BlockSpec((1, d), lambda i, j: (0, 0)),
                pl.BlockSpec((None, d, tf), lambda i, j: (layer, 0, j)),
                pl.BlockSpec((None, tf, d), lambda i, j: (layer, j, 0)),
                pl.BlockSpec((1, d), lambda i, j: (0, 0))]
    out_shape = [jax.ShapeDtypeStruct((n, d), F32)]
    out_specs = [pl.BlockSpec((tm, d), lambda i, j: (i, 0))]
    c_ins, c_in_specs, c_out_shape, c_out_specs = _cast_specs(
        cast_jobs, gi, gj, lambda i, j: (i, j))
    ins, in_specs = ins + c_ins, in_specs + c_in_specs
    out_shape, out_specs = out_shape + c_out_shape, out_specs + c_out_specs
    return pl.pallas_call(
        functools.partial(_mlp_kernel, final_norm=final_norm, n_cast=len(cast_jobs)),
        grid=(gi, gj), in_specs=in_specs, out_specs=out_specs, out_shape=out_shape,
        scratch_shapes=[pltpu.VMEM((tm, d), BF16)],
        compiler_params=_cparams(("parallel", "arbitrary")))(*ins)


def _wkv_kernel(*refs, has_v, n_cast):
    r_ref, k_ref, v_ref, lw_ref, a_ref, gate_ref = refs[:6]
    nxt = 6
    if has_v:
        vf_ref, vg_ref = refs[6:8]
        nxt = 8
    kk_ref, ka_ref, rk_ref, gnw_ref, gnb_ref = refs[nxt:nxt + 5]
    o_ref = refs[nxt + 5 + n_cast]
    st_ref = refs[-1]
    _run_casts(refs[nxt + 5:nxt + 5 + n_cast], refs[nxt + 6 + n_cast:-1])

    @pl.when(pl.program_id(2) == 0)
    def _():
        st_ref[...] = jnp.zeros_like(st_ref)

    L = WKV_CHUNK
    n2 = 2 * L
    half = RWKV_HEAD
    m0 = lax.broadcasted_iota(jnp.int32, (L, LANES), 1) < half
    zeros = jnp.zeros((L, LANES), F32)
    tri = (lax.broadcasted_iota(jnp.int32, (L, L), 1)
           <= lax.broadcasted_iota(jnp.int32, (L, L), 0)).astype(BF16)

    def h0(x):
        return jnp.where(m0, x, 0.0)

    def h1(x):
        return jnp.where(m0, 0.0, x)

    def seg_sum(x):
        s0 = jnp.sum(h0(x), axis=-1, keepdims=True)
        s1 = jnp.sum(h1(x), axis=-1, keepdims=True)
        return jnp.where(m0, s0, s1)

    def vcat(*xs):
        return jnp.concatenate(xs, axis=0)

    row = lax.broadcasted_iota(jnp.int32, (n2, n2), 0)
    col = lax.broadcasted_iota(jnp.int32, (n2, n2), 1)
    keep = (col % L) < (row % L) + jnp.where(row < L, 0, 1)
    same_head = (row // L) == (col // L)
    steps = int(math.log2(L))

    def pair_stages(sub, p):
        sl = slice(p * LANES, (p + 1) * LANES)
        rs = slice(sub * L, (sub + 1) * L)
        r, k, v, lw = r_ref[rs, sl], k_ref[rs, sl], v_ref[rs, sl], lw_ref[rs, sl]
        a = a_ref[rs, sl].astype(F32)
        if has_v:
            v = v + (vf_ref[rs, sl] - v) * vg_ref[rs, sl].astype(F32)
        kq = k * kk_ref[:, sl]
        kk = kq * lax.rsqrt(jnp.maximum(seg_sum(kq * kq), 1e-24))
        k2 = k * (1.0 + (a - 1.0) * ka_ref[:, sl])
        b = kk * a
        lw_hi = lw.astype(BF16)
        lw_lo = (lw - lw_hi.astype(F32)).astype(BF16)
        c = (jnp.dot(tri, lw_hi, preferred_element_type=F32)
             + jnp.dot(tri, lw_lo, preferred_element_type=F32))
        c_last = c[L - 1:L, :]
        e_neg = jnp.exp(-c)
        e_last = jnp.exp(c_last - c)
        at, rt, bt, kt = -kk * jnp.exp(c - lw), r * jnp.exp(c), b * e_neg, k2 * e_neg
        vr = pltpu.roll(v, half, axis=1)
        bk = vcat(b * e_last, k2 * e_last)
        yield
        a0 = jnp.where(keep, _dot_nt(vcat(h0(at), h0(rt)), vcat(bt, kt)), 0.0)
        a1 = jnp.where(keep, _dot_nt(vcat(h1(at), h1(rt)), vcat(kt, bt)), 0.0)
        yield
        x = vcat(h0(at) + _dot(a0[:L], vcat(zeros, h1(vr))),
                 h1(at) + _dot(a1[:L], vcat(h0(vr), zeros)))
        pcat = jnp.where(m0, a0[:L], a1[:L])
        yield
        for s in range(steps):
            pbd = vcat(h0(pcat), h1(pcat))
            x = x + _dot(pbd, x)
            if s + 1 < steps:
                pcat = _dot(pcat, pbd)
            yield
        wcat = jnp.where(m0, x[:L], x[L:])
        u0 = pltpu.roll(jnp.where(m0, x[L:], x[:L]), half, axis=1)
        wh = _dot_nt(vcat(wcat, rt), st_ref[p])
        yield
        u = wh[:L] + u0
        y = wh[L:] + _dot(jnp.concatenate([a0[L:], a1[L:]], axis=1),
                          vcat(h0(u), h0(v), h1(v), h1(u)))
        st_ref[p] = st_ref[p] * jnp.exp(c_last) + jnp.where(
            same_head, _dot_tn(vcat(u, v), bk), 0.0)
        yield
        mean = seg_sum(y) * (1.0 / half)
        dlt = y - mean
        var = seg_sum(dlt * dlt) * (1.0 / half)
        yn = dlt * lax.rsqrt(var + GN_EPS) * gnw_ref[:, sl] + gnb_ref[:, sl]
        bonus = seg_sum(r * k2 * rk_ref[:, sl]) * v
        o_ref[rs, sl] = ((yn + bonus) * gate_ref[rs, sl].astype(F32)).astype(BF16)

    n_pairs = o_ref.shape[1] // LANES
    n_sub = o_ref.shape[0] // L
    plan = {}
    for sub in range(n_sub):
        t0 = sub * WKV_CHAIN
        for p in range(n_pairs):
            spread = 1 + (p * WKV_CHAIN) // n_pairs
            first = t0 if sub == 0 else t0 - WKV_CHAIN + spread
            last = t0 + WKV_CHAIN + (1 if sub == n_sub - 1 else spread)
            gen = pair_stages(sub, p)
            for order, tick in enumerate([first] + [t0 + s for s in range(1, WKV_CHAIN + 1)] + [last]):
                plan.setdefault(tick, []).append((order in (0, WKV_CHAIN + 1), sub, p, gen))
    for tick in sorted(plan):
        for _, _, _, gen in sorted(plan[tick], key=lambda e: e[:3]):
            next(gen, None)


def _wkv(rkv, lw, a, gate, v_first, v_gate, k_k, k_a, r_k, gn_w, gn_b, batch, cast_jobs=(),
         width=2048):
    _, n, d = rkv.shape
    width = min(width, d)
    assert not cast_jobs or width == d
    L = WKV_CHUNK * WKV_SUBCHUNKS
    nc = n // batch // L
    has_v = v_first is not None
    tile = pl.BlockSpec((L, width), lambda b, p, c: (b * nc + c, p))
    rkv_spec = lambda q: pl.BlockSpec((None, L, width), lambda b, p, c: (q, b * nc + c, p))
    prm = pl.BlockSpec((1, width), lambda b, p, c: (0, p))
    ins = [rkv, rkv, rkv, lw, a, gate]
    in_specs = [rkv_spec(0), rkv_spec(1), rkv_spec(2), tile, tile, tile]
    if has_v:
        ins += [v_first, v_gate]
        in_specs += [rkv_spec(2), tile]
    ins += [x.reshape(1, d) for x in (k_k, k_a, r_k, gn_w, gn_b)]
    in_specs += [prm] * 5
    c_ins, c_in_specs, c_out_shape, c_out_specs = _cast_specs(
        cast_jobs, batch, nc, lambda b, p, c: (b, c))
    return pl.pallas_call(
        functools.partial(_wkv_kernel, has_v=has_v, n_cast=len(cast_jobs)),
        grid=(batch, d // width, nc), in_specs=in_specs + c_in_specs,
        out_specs=[tile] + c_out_specs,
        out_shape=[jax.ShapeDtypeStruct((n, d), BF16)] + c_out_shape,
        scratch_shapes=[pltpu.VMEM((width // LANES, LANES, LANES), F32)],
        compiler_params=_cparams(("parallel", "parallel", "arbitrary")))(*ins, *c_ins)


def _moba_kernel(q_ref, k_ref, v_ref, o_ref):
    bs = MOBA_BLOCK
    t = k_ref.shape[0]
    nb = t // bs
    nbp = 16
    qscale = (ATT_HEAD ** -0.5) * math.log2(math.e)
    kf = k_ref[...]
    kmean = jnp.mean(kf.reshape(nb, bs, ATT_HEAD), axis=1)
    kmean = jnp.concatenate([kmean, jnp.zeros((nbp - nb, ATT_HEAD), F32)], axis=0)
    km1, km2, _ = _split3(kmean)
    key_blk = lax.broadcasted_iota(jnp.int32, (t, LANES), 0) // bs
    key_lane = lax.broadcasted_iota(jnp.int32, (t, LANES), 1)
    k_aug = jnp.concatenate(
        [kf.astype(BF16), jnp.where(key_blk == key_lane, MASKED_LOGIT, 0.0).astype(BF16)], axis=1)
    v_aug = jnp.concatenate(
        [v_ref[...].astype(BF16), jnp.where(key_lane == 0, 1.0, 0.0).astype(BF16)], axis=1)
    eye = (lax.broadcasted_iota(jnp.int32, (nbp, LANES), 0)
           == lax.broadcasted_iota(jnp.int32, (nbp, LANES), 1)).astype(BF16)
    group = q_ref.shape[1] // ATT_HEAD
    rows = group * bs
    blk = lax.broadcasted_iota(jnp.int32, (nbp, rows), 0)
    qi = lax.broadcasted_iota(jnp.int32, (rows, bs), 0) % bs
    ki = lax.broadcasted_iota(jnp.int32, (rows, bs), 1)
    causal_bias = jnp.where(ki <= qi, 0.0, MASKED_LOGIT)
    for qb in range(nb):
        own = slice(qb * bs, (qb + 1) * bs)
        q = jnp.concatenate([q_ref[own, g * ATT_HEAD:(g + 1) * ATT_HEAD] for g in range(group)],
                            axis=0)
        qs = (q * qscale).astype(BF16)
        l_own = _dot_nt(qs, k_aug[own, :ATT_HEAD]) + causal_bias
        m = jnp.max(l_own, axis=-1, keepdims=True)
        if qb > 0:
            if qb > MOBA_TOPK:
                q1, q2, _ = _split3(q)
                gate = _dot_nt(km1, q1) + _dot_nt(km1, q2) + _dot_nt(km2, q1)
                cnt = jnp.zeros((nbp, rows), jnp.int32)
                for mb in range(qb):
                    gm = gate[mb:mb + 1, :]
                    beats = (gm > gate) | ((gm == gate) & (mb < blk))
                    cnt = cnt + beats.astype(jnp.int32)
                notsel = jnp.where((cnt >= MOBA_TOPK) & (blk < qb), 1.0, 0.0)
                flags = _dot_tn(notsel, eye).astype(BF16)
            else:
                flags = jnp.zeros((rows, LANES), BF16)
            l_past = _dot_nt(jnp.concatenate([qs, flags], axis=1), k_aug[:qb * bs])
            m = jnp.maximum(m, jnp.max(l_past, axis=-1, keepdims=True))
            acc = (_dot(jnp.exp2(l_past - m), v_aug[:qb * bs]) + _dot(jnp.exp2(l_own - m), v_aug[own]))
        else:
            acc = _dot(jnp.exp2(l_own - m), v_aug[own])
        out = (acc[:, :ATT_HEAD] / acc[:, ATT_HEAD:ATT_HEAD + 1]).astype(BF16)
        for g in range(group):
            o_ref[own, g * ATT_HEAD:(g + 1) * ATT_HEAD] = out[g * bs:(g + 1) * bs]


def _moba(q, kv, batch):
    n, dq = q.shape
    t = n // batch
    gw = dq // KV_HEADS
    return pl.pallas_call(
        _moba_kernel, grid=(batch, KV_HEADS),
        in_specs=[pl.BlockSpec((t, gw), lambda b, kh: (b, kh)),
                  pl.BlockSpec((t, ATT_HEAD), lambda b, kh: (b, kh)),
                  pl.BlockSpec((t, ATT_HEAD), lambda b, kh: (b, KV_HEADS + kh))],
        out_specs=pl.BlockSpec((t, gw), lambda b, kh: (b, kh)),
        out_shape=jax.ShapeDtypeStruct((n, dq), BF16),
        compiler_params=_cparams(("parallel", "parallel")))(q, kv, kv)


def _pad_lora(w_in, w_out):
    r = w_in.shape[1]
    rp = -(-r // LANES) * LANES
    return (jnp.pad(w_in, ((0, 0), (0, rp - r))).astype(BF16),
            jnp.pad(w_out, ((0, rp - r), (0, 0))).astype(BF16))


def _rope_tables(t):
    half = ATT_HEAD // 2
    inv = ROPE_THETA ** (-jnp.arange(half, dtype=F32) / half)
    ang = jnp.arange(t, dtype=F32)[:, None] * inv[None, :]
    cos, sin = jnp.cos(ang), jnp.sin(ang)
    return jnp.concatenate([cos, cos], axis=-1), jnp.concatenate([-sin, sin], axis=-1)


def kernel(x, ln_mix_g, ln_ffn_g, w_ff1, w_ff2, rw_mu, rw_w_rkv, rw_w0, rw_w1, rw_w2, rw_a0, rw_a1, rw_a2, rw_g1, rw_g2, rw_k_k, rw_k_a, rw_r_k, rw_gn_w, rw_gn_b, rw_w_o, rw_v0, rw_v1, rw_v2, kv_norm_g, w_kv, mb_w_q, mb_w_o, final_g):
    batch, t, d = x.shape
    n = batch * t
    depth = ln_mix_g.shape[0]
    n_rwkv = rw_mu.shape[0]
    h = x.reshape(n, d)
    cos, sin = _rope_tables(t)
    rkv_f32 = rw_w_rkv.reshape(n_rwkv, 3 * d, d)
    v_first = None
    kv = None
    for layer in range(depth):
        if layer < n_rwkv:
            i = layer
            w1, w2 = _pad_lora(rw_w1[i], rw_w2[i])
            a1, a2 = _pad_lora(rw_a1[i], rw_a2[i])
            g1, g2 = _pad_lora(rw_g1[i], rw_g2[i])
            v_lora = None
            if i > 0:
                v1, v2 = _pad_lora(rw_v1[i - 1], rw_v2[i - 1])
                v_lora = (rw_v0[i - 1], v1, v2)
            first = layer == 0
            prep = _rwkv_prep(h, t, ln_mix_g[layer], rw_mu[i], rw_w0[i], w1, w2,
                              rw_a0[i], a1, a2, g1, g2, v_lora,
                              [(rkv_f32, 0)] if first else [])
            x3, lw, a, gate = prep[:4]
            v_gate = prep[4] if i > 0 else None
            if first:
                rkv_b = prep[-1]
            rkv = _mm3(x3, rkv_b.reshape(1, 3, d, d), 0)
            res = _wkv(rkv, lw, a, gate, v_first, v_gate, rw_k_k[i], rw_k_a[i], rw_r_k[i],
                       rw_gn_w[i], rw_gn_b[i], batch,
                       [(w_ff1, 0), (w_ff2, 0), (rw_w_o, 0)] if first else [])
            yg = res[0]
            if first:
                w1_b, w2_b, wo_b = res[1:]
            if i == 0:
                v_first = rkv
            h = _mm_res(yg, wo_b, 0, h)
        else:
            if kv is None:
                kv = _norm_mm_rope(h, kv_norm_g, wkv_b, 0, cos, sin, KV_HEADS * ATT_HEAD, t)
            q = _norm_mm_rope(h, ln_mix_g[layer], wq_b, 0, cos, sin, mb_w_q.shape[2], t)
            att = _moba(q, kv, batch)
            h = _mm_res(att, wo_b, 0, h)
        nxt = layer + 1
        jobs = []
        if nxt < depth:
            jobs = [(w_ff1, nxt), (w_ff2, nxt)]
            if nxt < n_rwkv:
                jobs += [(rkv_f32, nxt), (rw_w_o, nxt)]
            else:
                jobs += [(mb_w_q, nxt - n_rwkv), (mb_w_o, nxt - n_rwkv)]
                if nxt == n_rwkv:
                    jobs.append((w_kv[None], 0))
        res = _mlp(h, ln_ffn_g[layer], w1_b, w2_b, 0, final_g, nxt == depth, jobs)
        h = res[0]
        if jobs:
            w1_b, w2_b = res[1:3]
            if nxt < n_rwkv:
                rkv_b, wo_b = res[3:5]
            else:
                wq_b, wo_b = res[3:5]
                if nxt == n_rwkv:
                    wkv_b = res[5]
    return h.reshape(batch, t, d)
```

```python
import functools
import math

import jax
import jax.numpy as jnp
from jax import lax
from jax.experimental import pallas as pl
from jax.experimental.pallas import tpu as pltpu

F32 = jnp.float32
BF16 = jnp.bfloat16

RMS_EPS = 1e-6
GN_EPS = 64e-5
RWKV_HEAD = 64
ATT_HEAD = 128
KV_HEADS = 4
MOBA_BLOCK = 256
MOBA_TOPK = 3
ROPE_THETA = 10000.0
LANES = 128
WKV_CHUNK = 64
WKV_CHAIN = 10
WKV_SUBCHUNKS = 4
MASKED_LOGIT = -1e30
VMEM_LIMIT = 56 * 1024 * 1024


def _cparams(sem):
    return pltpu.CompilerParams(dimension_semantics=sem, vmem_limit_bytes=VMEM_LIMIT)


def _rms(x, g):
    return x * lax.rsqrt(jnp.mean(x * x, axis=-1, keepdims=True) + RMS_EPS) * g


def _sigmoid(x):
    return 0.5 * jnp.tanh(0.5 * x) + 0.5


def _dot(a, b):
    return jnp.dot(a.astype(BF16), b.astype(BF16), preferred_element_type=F32)


def _dot_nt(a, b):
    return lax.dot_general(a.astype(BF16), b.astype(BF16), (((1,), (1,)), ((), ())),
                           preferred_element_type=F32)


def _dot_tn(a, b):
    return lax.dot_general(a.astype(BF16), b.astype(BF16), (((0,), (0,)), ((), ())),
                           preferred_element_type=F32)


def _split3(x):
    h1 = x.astype(BF16)
    r1 = x - h1.astype(F32)
    h2 = r1.astype(BF16)
    h3 = (r1 - h2.astype(F32)).astype(BF16)
    return h1, h2, h3


def _cast_tile(rows, cols, gi, gj):
    if (rows // gi) % 16 == 0 and (cols // gj) % LANES == 0:
        return (rows // gi, cols // gj), lambda i, j: (i, j)
    assert (rows // gj) % 16 == 0 and (cols // gi) % LANES == 0, (rows, cols, gi, gj)
    return (rows // gj, cols // gi), lambda i, j: (j, i)


def _cast_specs(cast_jobs, gi, gj, pick):
    ins, in_specs, out_shape, out_specs = [], [], [], []
    for w, idx in cast_jobs:
        _, rows, cols = w.shape
        blk, tile_index = _cast_tile(rows, cols, gi, gj)
        ins.append(w)
        in_specs.append(pl.BlockSpec(
            (None,) + blk, lambda *g, idx=idx, t=tile_index: (idx,) + t(*pick(*g))))
        out_shape.append(jax.ShapeDtypeStruct((1, rows, cols), BF16))
        out_specs.append(pl.BlockSpec((None,) + blk, lambda *g, t=tile_index: (0,) + t(*pick(*g))))
    return ins, in_specs, out_shape, out_specs


def _run_casts(src_refs, dst_refs):
    for src, dst in zip(src_refs, dst_refs):
        dst[...] = src[...].astype(BF16)


def _rwkv_prep_kernel(*refs, tiles_per_seq, has_v, n_cast):
    (h_ref, hprev_ref, g_ref, mu_ref, w0_ref, w1_ref, w2_ref, a0_ref, a1_ref, a2_ref,
     g1_ref, g2_ref) = refs[:12]
    n_in = 15 if has_v else 12
    n_out = 5 if has_v else 4
    if has_v:
        v0_ref, v1_ref, v2_ref = refs[12:15]
    outs = refs[n_in + n_cast:n_in + n_cast + n_out]
    x3_ref, lw_ref, a_ref, gate_ref = outs[:4]
    _run_casts(refs[n_in:n_in + n_cast], refs[n_in + n_cast + n_out:])
    i = pl.program_id(0)
    g = g_ref[...]
    hn = _rms(h_ref[...], g)
    prev = _rms(hprev_ref[7:8, :], g)
    prev = jnp.where(i % tiles_per_seq == 0, 0.0, prev)
    rows = lax.broadcasted_iota(jnp.int32, hn.shape, 0)
    shifted = jnp.where(rows == 0, prev, pltpu.roll(hn, 1, axis=0))
    xx = shifted - hn
    x3_ref[0] = (hn + xx * mu_ref[0:1, :]).astype(BF16)
    x3_ref[1] = (hn + xx * mu_ref[2:3, :]).astype(BF16)
    xv = (hn + xx * mu_ref[3:4, :]).astype(BF16)
    x3_ref[2] = xv
    xw = hn + xx * mu_ref[1:2, :]
    z = w0_ref[...] + _dot(jnp.tanh(_dot(xw, w1_ref[...])), w2_ref[...])
    lw_ref[...] = -math.exp(-0.5) * _sigmoid(z)
    xa = hn + xx * mu_ref[4:5, :]
    a_ref[...] = _sigmoid(a0_ref[...] + _dot(_dot(xa, a1_ref[...]), a2_ref[...])).astype(BF16)
    xg = hn + xx * mu_ref[5:6, :]
    gate_ref[...] = _dot(_sigmoid(_dot(xg, g1_ref[...])), g2_ref[...]).astype(BF16)
    if has_v:
        outs[4][...] = _sigmoid(
            v0_ref[...] + _dot(_dot(xv, v1_ref[...]), v2_ref[...])).astype(BF16)


def _rwkv_prep(h, seq_len, g, mu, w0, w1, w2, a0, a1, a2, g1, g2, v_lora, cast_jobs=(), tm=256):
    n, d = h.shape
    has_v = v_lora is not None
    row = lambda v: v.reshape(1, d)
    full = lambda arr: pl.BlockSpec(arr.shape, lambda i: (0,) * arr.ndim)
    tile = pl.BlockSpec((tm, d), lambda i: (i, 0))
    ins = [h, h, row(g), mu, row(w0), w1, w2, row(a0), a1, a2, g1, g2]
    in_specs = [tile, pl.BlockSpec((8, d), lambda i: (jnp.maximum(i * (tm // 8) - 1, 0), 0))]
    in_specs += [full(x) for x in ins[2:]]
    out_shape = [jax.ShapeDtypeStruct((3, n, d), BF16), jax.ShapeDtypeStruct((n, d), F32),
                 jax.ShapeDtypeStruct((n, d), BF16), jax.ShapeDtypeStruct((n, d), BF16)]
    out_specs = [pl.BlockSpec((3, tm, d), lambda i: (0, i, 0)), tile, tile, tile]
    if has_v:
        v0, v1, v2 = v_lora
        extra = [row(v0), v1, v2]
        ins += extra
        in_specs += [full(x) for x in extra]
        out_shape.append(jax.ShapeDtypeStruct((n, d), BF16))
        out_specs.append(tile)
    c_ins, c_in_specs, c_out_shape, c_out_specs = _cast_specs(
        cast_jobs, n // tm, 1, lambda i: (i, 0))
    ins, in_specs = ins + c_ins, in_specs + c_in_specs
    out_shape, out_specs = out_shape + c_out_shape, out_specs + c_out_specs
    return pl.pallas_call(
        functools.partial(_rwkv_prep_kernel, tiles_per_seq=seq_len // tm, has_v=has_v,
                          n_cast=len(cast_jobs)),
        grid=(n // tm,), in_specs=in_specs, out_specs=out_specs, out_shape=out_shape,
        compiler_params=_cparams(("parallel",)))(*ins)


def _mm3_kernel(x_ref, w_ref, o_ref):
    o_ref[...] = jnp.dot(x_ref[...], w_ref[...], preferred_element_type=F32)


def _mm3(x3, w4, layer, tm=1024, tn=2048):
    p, n, k = x3.shape
    m = w4.shape[3]
    tm, tn = min(tm, n), min(tn, m)
    return pl.pallas_call(
        _mm3_kernel, grid=(p, n // tm, m // tn),
        in_specs=[pl.BlockSpec((None, tm, k), lambda q, i, j: (q, i, 0)),
                  pl.BlockSpec((None, None, k, tn), lambda q, i, j: (layer, q, 0, j))],
        out_specs=pl.BlockSpec((None, tm, tn), lambda q, i, j: (q, i, j)),
        out_shape=jax.ShapeDtypeStruct((p, n, m), F32),
        compiler_params=_cparams(("parallel", "parallel", "arbitrary")))(x3, w4)


def _mm_res_kernel(x_ref, w_ref, r_ref, o_ref):
    o_ref[...] = r_ref[...] + jnp.dot(x_ref[...], w_ref[...], preferred_element_type=F32)


def _mm_res(x, w3, layer, res, tm=512, tn=2048):
    n, k = x.shape
    m = w3.shape[2]
    tm, tn = min(tm, n), min(tn, m)
    return pl.pallas_call(
        _mm_res_kernel, grid=(n // tm, m // tn),
        in_specs=[pl.BlockSpec((tm, k), lambda i, j: (i, 0)),
                  pl.BlockSpec((None, k, tn), lambda i, j: (layer, 0, j)),
                  pl.BlockSpec((tm, tn), lambda i, j: (i, j))],
        out_specs=pl.BlockSpec((tm, tn), lambda i, j: (i, j)),
        out_shape=jax.ShapeDtypeStruct((n, m), F32),
        compiler_params=_cparams(("parallel", "arbitrary")))(x, w3, res)


def _norm_mm_rope_kernel(h_ref, g_ref, w_ref, cos_ref, sin_ref, o_ref, hn_ref, *, rope_tiles):
    j = pl.program_id(1)

    @pl.when(j == 0)
    def _():
        hn_ref[...] = _rms(h_ref[...], g_ref[...]).astype(BF16)

    acc = jnp.dot(hn_ref[...], w_ref[...], preferred_element_type=F32)

    @pl.when(j < rope_tiles)
    def _():
        cos, sin = cos_ref[...], sin_ref[...]
        for hh in range(acc.shape[1] // ATT_HEAD):
            x = acc[:, hh * ATT_HEAD:(hh + 1) * ATT_HEAD]
            o_ref[:, hh * ATT_HEAD:(hh + 1) * ATT_HEAD] = (
                x * cos + pltpu.roll(x, ATT_HEAD // 2, axis=1) * sin)

    @pl.when(j >= rope_tiles)
    def _():
        o_ref[...] = acc


def _norm_mm_rope(h, g, w3, layer, cos, sin, rope_cols, seq_len, tm=1024, tn=1024):
    n, d = h.shape
    m = w3.shape[2]
    tm, tn = min(tm, seq_len), min(tn, rope_cols)
    return pl.pallas_call(
        functools.partial(_norm_mm_rope_kernel, rope_tiles=rope_cols // tn),
        grid=(n // tm, m // tn),
        in_specs=[pl.BlockSpec((tm, d), lambda i, j: (i, 0)),
                  pl.BlockSpec((1, d), lambda i, j: (0, 0)),
                  pl.BlockSpec((None, d, tn), lambda i, j: (layer, 0, j)),
                  pl.BlockSpec((tm, ATT_HEAD), lambda i, j: (i % (seq_len // tm), 0)),
                  pl.BlockSpec((tm, ATT_HEAD), lambda i, j: (i % (seq_len // tm), 0))],
        out_specs=pl.BlockSpec((tm, tn), lambda i, j: (i, j)),
        out_shape=jax.ShapeDtypeStruct((n, m), F32),
        scratch_shapes=[pltpu.VMEM((tm, d), BF16)],
        compiler_params=_cparams(("parallel", "arbitrary")))(h, g.reshape(1, d), w3, cos, sin)


def _mlp_kernel(*refs, final_norm, n_cast):
    h_ref, g_ref, w1_ref, w2_ref, gf_ref = refs[:5]
    src_refs = refs[5:5 + n_cast]
    o_ref = refs[5 + n_cast]
    dst_refs = refs[6 + n_cast:6 + 2 * n_cast]
    hn_ref = refs[6 + 2 * n_cast]
    _run_casts(src_refs, dst_refs)
    j = pl.program_id(1)

    @pl.when(j == 0)
    def _():
        h = h_ref[...]
        hn_ref[...] = _rms(h, g_ref[...]).astype(BF16)
        o_ref[...] = h

    u = jnp.dot(hn_ref[...], w1_ref[...], preferred_element_type=F32)
    u = jnp.square(jnp.maximum(u, 0.0)).astype(BF16)
    o_ref[...] += jnp.dot(u, w2_ref[...], preferred_element_type=F32)

    if final_norm:
        @pl.when(j == pl.num_programs(1) - 1)
        def _():
            o_ref[...] = _rms(o_ref[...], gf_ref[...])


def _mlp(h, g, w1, w2, layer, gf, final_norm, cast_jobs=(), tm=1024, tf=512):
    n, d = h.shape
    f = w1.shape[2]
    tm = min(tm, n)
    gi, gj = n // tm, f // tf
    ins = [h, g.reshape(1, d), w1, w2, gf.reshape(1, d)]
    in_specs = [pl.BlockSpec((tm, d), lambda i, j: (i, 0)),
                pl.BlockSpec((1, d), lambda i, j: (0, 0)),
                pl.BlockSpec((None, d, tf), lambda i, j: (layer, 0, j)),
                pl.BlockSpec((None, tf, d), lambda i, j: (layer, j, 0)),
                pl.BlockSpec((1, d), lambda i, j: (0, 0))]
    out_shape = [jax.ShapeDtypeStruct((n, d), F32)]
    out_specs = [pl.BlockSpec((tm, d), lambda i, j: (i, 0))]
    c_ins, c_in_specs, c_out_shape, c_out_specs = _cast_specs(
        cast_jobs, gi, gj, lambda i, j: (i, j))
    ins, in_specs = ins + c_ins, in_specs + c_in_specs
    out_shape, out_specs = out_shape + c_out_shape, out_specs + c_out_specs
    return pl.pallas_call(
        functools.partial(_mlp_kernel, final_norm=final_norm, n_cast=len(cast_jobs)),
        grid=(gi, gj), in_specs=in_specs, out_specs=out_specs, out_shape=out_shape,
        scratch_shapes=[pltpu.VMEM((tm, d), BF16)],
        compiler_params=_cparams(("parallel", "arbitrary")))(*ins)


def _wkv_kernel(*refs, has_v, n_cast):
    r_ref, k_ref, v_ref, lw_ref, a_ref, gate_ref = refs[:6]
    nxt = 6
    if has_v:
        vf_ref, vg_ref = refs[6:8]
        nxt = 8
    kk_ref, ka_ref, rk_ref, gnw_ref, gnb_ref = refs[nxt:nxt + 5]
    o_ref = refs[nxt + 5 + n_cast]
    st_ref = refs[-1]
    _run_casts(refs[nxt + 5:nxt + 5 + n_cast], refs[nxt + 6 + n_cast:-1])

    @pl.when(pl.program_id(2) == 0)
    def _():
        st_ref[...] = jnp.zeros_like(st_ref)

    L = WKV_CHUNK
    n2 = 2 * L
    half = RWKV_HEAD
    m0 = lax.broadcasted_iota(jnp.int32, (L, LANES), 1) < half
    zeros = jnp.zeros((L, LANES), F32)
    tri = (lax.broadcasted_iota(jnp.int32, (L, L), 1)
           <= lax.broadcasted_iota(jnp.int32, (L, L), 0)).astype(BF16)

    def h0(x):
        return jnp.where(m0, x, 0.0)

    def h1(x):
        return jnp.where(m0, 0.0, x)

    def seg_sum(x):
        s0 = jnp.sum(h0(x), axis=-1, keepdims=True)
        s1 = jnp.sum(h1(x), axis=-1, keepdims=True)
        return jnp.where(m0, s0, s1)

    def vcat(*xs):
        return jnp.concatenate(xs, axis=0)

    row = lax.broadcasted_iota(jnp.int32, (n2, n2), 0)
    col = lax.broadcasted_iota(jnp.int32, (n2, n2), 1)
    keep = (col % L) < (row % L) + jnp.where(row < L, 0, 1)
    same_head = (row // L) == (col // L)
    steps = int(math.log2(L))

    def pair_stages(sub, p):
        sl = slice(p * LANES, (p + 1) * LANES)
        rs = slice(sub * L, (sub + 1) * L)
        r, k, v, lw = r_ref[rs, sl], k_ref[rs, sl], v_ref[rs, sl], lw_ref[rs, sl]
        a = a_ref[rs, sl].astype(F32)
        if has_v:
            v = v + (vf_ref[rs, sl] - v) * vg_ref[rs, sl].astype(F32)
        kq = k * kk_ref[:, sl]
        kk = kq * lax.rsqrt(jnp.maximum(seg_sum(kq * kq), 1e-24))
        k2 = k * (1.0 + (a - 1.0) * ka_ref[:, sl])
        b = kk * a
        lw_hi = lw.astype(BF16)
        lw_lo = (lw - lw_hi.astype(F32)).astype(BF16)
        c = (jnp.dot(tri, lw_hi, preferred_element_type=F32)
             + jnp.dot(tri, lw_lo, preferred_element_type=F32))
        c_last = c[L - 1:L, :]
        e_neg = jnp.exp(-c)
        e_last = jnp.exp(c_last - c)
        at, rt, bt, kt = -kk * jnp.exp(c - lw), r * jnp.exp(c), b * e_neg, k2 * e_neg
        vr = pltpu.roll(v, half, axis=1)
        bk = vcat(b * e_last, k2 * e_last)
        yield
        a0 = jnp.where(keep, _dot_nt(vcat(h0(at), h0(rt)), vcat(bt, kt)), 0.0)
        a1 = jnp.where(keep, _dot_nt(vcat(h1(at), h1(rt)), vcat(kt, bt)), 0.0)
        yield
        x = vcat(h0(at) + _dot(a0[:L], vcat(zeros, h1(vr))),
                 h1(at) + _dot(a1[:L], vcat(h0(vr), zeros)))
        pcat = jnp.where(m0, a0[:L], a1[:L])
        yield
        for s in range(steps):
            pbd = vcat(h0(pcat), h1(pcat))
            x = x + _dot(pbd, x)
            if s + 1 < steps:
                pcat = _dot(pcat, pbd)
            yield
        wcat = jnp.where(m0, x[:L], x[L:])
        u0 = pltpu.roll(jnp.where(m0, x[L:], x[:L]), half, axis=1)
        wh = _dot_nt(vcat(wcat, rt), st_ref[p])
        yield
        u = wh[:L] + u0
        y = wh[L:] + _dot(jnp.concatenate([a0[L:], a1[L:]], axis=1),
                          vcat(h0(u), h0(v), h1(v), h1(u)))
        st_ref[p] = st_ref[p] * jnp.exp(c_last) + jnp.where(
            same_head, _dot_tn(vcat(u, v), bk), 0.0)
        yield
        mean = seg_sum(y) * (1.0 / half)
        dlt = y - mean
        var = seg_sum(dlt * dlt) * (1.0 / half)
        yn = dlt * lax.rsqrt(var + GN_EPS) * gnw_ref[:, sl] + gnb_ref[:, sl]
        bonus = seg_sum(r * k2 * rk_ref[:, sl]) * v
        o_ref[rs, sl] = ((yn + bonus) * gate_ref[rs, sl].astype(F32)).astype(BF16)

    n_pairs = o_ref.shape[1] // LANES
    n_sub = o_ref.shape[0] // L
    plan = {}
    for sub in range(n_sub):
        t0 = sub * WKV_CHAIN
        for p in range(n_pairs):
            spread = 1 + (p * WKV_CHAIN) // n_pairs
            first = t0 if sub == 0 else t0 - WKV_CHAIN + spread
            last = t0 + WKV_CHAIN + (1 if sub == n_sub - 1 else spread)
            gen = pair_stages(sub, p)
            for order, tick in enumerate([first] + [t0 + s for s in range(1, WKV_CHAIN + 1)] + [last]):
                plan.setdefault(tick, []).append((order in (0, WKV_CHAIN + 1), sub, p, gen))
    for tick in sorted(plan):
        for _, _, _, gen in sorted(plan[tick], key=lambda e: e[:3]):
            next(gen, None)


def _wkv(rkv, lw, a, gate, v_first, v_gate, k_k, k_a, r_k, gn_w, gn_b, batch, cast_jobs=(),
         width=2048):
    _, n, d = rkv.shape
    width = min(width, d)
    assert not cast_jobs or width == d
    L = WKV_CHUNK * WKV_SUBCHUNKS
    nc = n // batch // L
    has_v = v_first is not None
    tile = pl.BlockSpec((L, width), lambda b, p, c: (b * nc + c, p))
    rkv_spec = lambda q: pl.BlockSpec((None, L, width), lambda b, p, c: (q, b * nc + c, p))
    prm = pl.BlockSpec((1, width), lambda b, p, c: (0, p))
    ins = [rkv, rkv, rkv, lw, a, gate]
    in_specs = [rkv_spec(0), rkv_spec(1), rkv_spec(2), tile, tile, tile]
    if has_v:
        ins += [v_first, v_gate]
        in_specs += [rkv_spec(2), tile]
    ins += [x.reshape(1, d) for x in (k_k, k_a, r_k, gn_w, gn_b)]
    in_specs += [prm] * 5
    c_ins, c_in_specs, c_out_shape, c_out_specs = _cast_specs(
        cast_jobs, batch, nc, lambda b, p, c: (b, c))
    return pl.pallas_call(
        functools.partial(_wkv_kernel, has_v=has_v, n_cast=len(cast_jobs)),
        grid=(batch, d // width, nc), in_specs=in_specs + c_in_specs,
        out_specs=[tile] + c_out_specs,
        out_shape=[jax.ShapeDtypeStruct((n, d), BF16)] + c_out_shape,
        scratch_shapes=[pltpu.VMEM((width // LANES, LANES, LANES), F32)],
        compiler_params=_cparams(("parallel", "parallel", "arbitrary")))(*ins, *c_ins)


def _moba_kernel(q_ref, k_ref, v_ref, o_ref):
    bs = MOBA_BLOCK
    t = k_ref.shape[0]
    nb = t // bs
    nbp = 16
    qscale = (ATT_HEAD ** -0.5) * math.log2(math.e)
    kf = k_ref[...]
    kmean = jnp.mean(kf.reshape(nb, bs, ATT_HEAD), axis=1)
    kmean = jnp.concatenate([kmean, jnp.zeros((nbp - nb, ATT_HEAD), F32)], axis=0)
    km1, km2, _ = _split3(kmean)
    key_blk = lax.broadcasted_iota(jnp.int32, (t, LANES), 0) // bs
    key_lane = lax.broadcasted_iota(jnp.int32, (t, LANES), 1)
    k_aug = jnp.concatenate(
        [kf.astype(BF16), jnp.where(key_blk == key_lane, MASKED_LOGIT, 0.0).astype(BF16)], axis=1)
    v_aug = jnp.concatenate(
        [v_ref[...].astype(BF16), jnp.where(key_lane == 0, 1.0, 0.0).astype(BF16)], axis=1)
    eye = (lax.broadcasted_iota(jnp.int32, (nbp, LANES), 0)
           == lax.broadcasted_iota(jnp.int32, (nbp, LANES), 1)).astype(BF16)
    group = q_ref.shape[1] // ATT_HEAD
    rows = group * bs
    blk = lax.broadcasted_iota(jnp.int32, (nbp, rows), 0)
    qi = lax.broadcasted_iota(jnp.int32, (rows, bs), 0) % bs
    ki = lax.broadcasted_iota(jnp.int32, (rows, bs), 1)
    causal_bias = jnp.where(ki <= qi, 0.0, MASKED_LOGIT)
    for qb in range(nb):
        own = slice(qb * bs, (qb + 1) * bs)
        q = jnp.concatenate([q_ref[own, g * ATT_HEAD:(g + 1) * ATT_HEAD] for g in range(group)],
                            axis=0)
        qs = (q * qscale).astype(BF16)
        l_own = _dot_nt(qs, k_aug[own, :ATT_HEAD]) + causal_bias
        m = jnp.max(l_own, axis=-1, keepdims=True)
        if qb > 0:
            if qb > MOBA_TOPK:
                q1, q2, _ = _split3(q)
                gate = _dot_nt(km1, q1) + _dot_nt(km1, q2) + _dot_nt(km2, q1)
                cnt = jnp.zeros((nbp, rows), jnp.int32)
                for mb in range(qb):
                    gm = gate[mb:mb + 1, :]
                    beats = (gm > gate) | ((gm == gate) & (mb < blk))
                    cnt = cnt + beats.astype(jnp.int32)
                notsel = jnp.where((cnt >= MOBA_TOPK) & (blk < qb), 1.0, 0.0)
                flags = _dot_tn(notsel, eye).astype(BF16)
            else:
                flags = jnp.zeros((rows, LANES), BF16)
            l_past = _dot_nt(jnp.concatenate([qs, flags], axis=1), k_aug[:qb * bs])
            m = jnp.maximum(m, jnp.max(l_past, axis=-1, keepdims=True))
            acc = (_dot(jnp.exp2(l_past - m), v_aug[:qb * bs]) + _dot(jnp.exp2(l_own - m), v_aug[own]))
        else:
            acc = _dot(jnp.exp2(l_own - m), v_aug[own])
        out = (acc[:, :ATT_HEAD] / acc[:, ATT_HEAD:ATT_HEAD + 1]).astype(BF16)
        for g in range(group):
            o_ref[own, g * ATT_HEAD:(g + 1) * ATT_HEAD] = out[g * bs:(g + 1) * bs]


def _moba(q, kv, batch):
    n, dq = q.shape
    t = n // batch
    gw = dq // KV_HEADS
    return pl.pallas_call(
        _moba_kernel, grid=(batch, KV_HEADS),
        in_specs=[pl.BlockSpec((t, gw), lambda b, kh: (b, kh)),
                  pl.BlockSpec((t, ATT_HEAD), lambda b, kh: (b, kh)),
                  pl.BlockSpec((t, ATT_HEAD), lambda b, kh: (b, KV_HEADS + kh))],
        out_specs=pl.BlockSpec((t, gw), lambda b, kh: (b, kh)),
        out_shape=jax.ShapeDtypeStruct((n, dq), BF16),
        compiler_params=_cparams(("parallel", "parallel")))(q, kv, kv)


def _pad_lora(w_in, w_out):
    r = w_in.shape[1]
    rp = -(-r // LANES) * LANES
    return (jnp.pad(w_in, ((0, 0), (0, rp - r))).astype(BF16),
            jnp.pad(w_out, ((0, rp - r), (0, 0))).astype(BF16))


def _rope_tables(t):
    half = ATT_HEAD // 2
    inv = ROPE_THETA ** (-jnp.arange(half, dtype=F32) / half)
    ang = jnp.arange(t, dtype=F32)[:, None] * inv[None, :]
    cos, sin = jnp.cos(ang), jnp.sin(ang)
    return jnp.concatenate([cos, cos], axis=-1), jnp.concatenate([-sin, sin], axis=-1)


def kernel(x, ln_mix_g, ln_ffn_g, w_ff1, w_ff2, rw_mu, rw_w_rkv, rw_w0, rw_w1, rw_w2, rw_a0, rw_a1, rw_a2, rw_g1, rw_g2, rw_k_k, rw_k_a, rw_r_k, rw_gn_w, rw_gn_b, rw_w_o, rw_v0, rw_v1, rw_v2, kv_norm_g, w_kv, mb_w_q, mb_w_o, final_g):
    batch, t, d = x.shape
    n = batch * t
    depth = ln_mix_g.shape[0]
    n_rwkv = rw_mu.shape[0]
    h = x.reshape(n, d)
    cos, sin = _rope_tables(t)
    rkv_f32 = rw_w_rkv.reshape(n_rwkv, 3 * d, d)
    v_first = None
    kv = None
    for layer in range(depth):
        if layer < n_rwkv:
            i = layer
            w1, w2 = _pad_lora(rw_w1[i], rw_w2[i])
            a1, a2 = _pad_lora(rw_a1[i], rw_a2[i])
            g1, g2 = _pad_lora(rw_g1[i], rw_g2[i])
            v_lora = None
            if i > 0:
                v1, v2 = _pad_lora(rw_v1[i - 1], rw_v2[i - 1])
                v_lora = (rw_v0[i - 1], v1, v2)
            first = layer == 0
            prep = _rwkv_prep(h, t, ln_mix_g[layer], rw_mu[i], rw_w0[i], w1, w2,
                              rw_a0[i], a1, a2, g1, g2, v_lora,
                              [(rkv_f32, 0)] if first else [])
            x3, lw, a, gate = prep[:4]
            v_gate = prep[4] if i > 0 else None
            if first:
                rkv_b = prep[-1]
            rkv = _mm3(x3, rkv_b.reshape(1, 3, d, d), 0)
            res = _wkv(rkv, lw, a, gate, v_first, v_gate, rw_k_k[i], rw_k_a[i], rw_r_k[i],
                       rw_gn_w[i], rw_gn_b[i], batch,
                       [(w_ff1, 0), (w_ff2, 0), (rw_w_o, 0)] if first else [])
            yg = res[0]
            if first:
                w1_b, w2_b, wo_b = res[1:]
            if i == 0:
                v_first = rkv
            h = _mm_res(yg, wo_b, 0, h)
        else:
            if kv is None:
                kv = _norm_mm_rope(h, kv_norm_g, wkv_b, 0, cos, sin, KV_HEADS * ATT_HEAD, t)
            q = _norm_mm_rope(h, ln_mix_g[layer], wq_b, 0, cos, sin, mb_w_q.shape[2], t)
            att = _moba(q, kv, batch)
            h = _mm_res(att, wo_b, 0, h)
        nxt = layer + 1
        jobs = []
        if nxt < depth:
            jobs = [(w_ff1, nxt), (w_ff2, nxt)]
            if nxt < n_rwkv:
                jobs += [(rkv_f32, nxt), (rw_w_o, nxt)]
            else:
                jobs += [(mb_w_q, nxt - n_rwkv), (mb_w_o, nxt - n_rwkv)]
                if nxt == n_rwkv:
                    jobs.append((w_kv[None], 0))
        res = _mlp(h, ln_ffn_g[layer], w1_b, w2_b, 0, final_g, nxt == depth, jobs)
        h = res[0]
        if jobs:
            w1_b, w2_b = res[1:3]
            if nxt < n_rwkv:
                rkv_b, wo_b = res[3:5]
            else:
                wq_b, wo_b = res[3:5]
                if nxt == n_rwkv:
                    wkv_b = res[5]
    return h.reshape(batch, t, d)
```

```python
import functools
import math

import jax
import jax.numpy as jnp
from jax import lax
from jax.experimental import pallas as pl
from jax.experimental.pallas import tpu as pltpu

F32 = jnp.float32
BF16 = jnp.bfloat16

RMS_EPS = 1e-6
GN_EPS = 64e-5
RWKV_HEAD = 64
ATT_HEAD = 128
KV_HEADS = 4
MOBA_BLOCK = 256
MOBA_TOPK = 3
ROPE_THETA = 10000.0
LANES = 128
WKV_CHUNK = 64
WKV_CHAIN = 10
WKV_SUBCHUNKS = 4
MASKED_LOGIT = -1e30
VMEM_LIMIT = 56 * 1024 * 1024


def _cparams(sem):
    return pltpu.CompilerParams(dimension_semantics=sem, vmem_limit_bytes=VMEM_LIMIT)


def _rms(x, g):
    return x * lax.rsqrt(jnp.mean(x * x, axis=-1, keepdims=True) + RMS_EPS) * g


def _sigmoid(x):
    return 0.5 * jnp.tanh(0.5 * x) + 0.5


def _dot(a, b):
    return jnp.dot(a.astype(BF16), b.astype(BF16), preferred_element_type=F32)


def _dot_nt(a, b):
    return lax.dot_general(a.astype(BF16), b.astype(BF16), (((1,), (1,)), ((), ())),
                           preferred_element_type=F32)


def _dot_tn(a, b):
    return lax.dot_general(a.astype(BF16), b.astype(BF16), (((0,), (0,)), ((), ())),
                           preferred_element_type=F32)


def _split3(x):
    h1 = x.astype(BF16)
    r1 = x - h1.astype(F32)
    h2 = r1.astype(BF16)
    h3 = (r1 - h2.astype(F32)).astype(BF16)
    return h1, h2, h3


def _cast_tile(rows, cols, gi, gj):
    if (rows // gi) % 16 == 0 and (cols // gj) % LANES == 0:
        return (rows // gi, cols // gj), lambda i, j: (i, j)
    assert (rows // gj) % 16 == 0 and (cols // gi) % LANES == 0, (rows, cols, gi, gj)
    return (rows // gj, cols // gi), lambda i, j: (j, i)


def _cast_specs(cast_jobs, gi, gj, pick):
    ins, in_specs, out_shape, out_specs = [], [], [], []
    for w, idx in cast_jobs:
        _, rows, cols = w.shape
        blk, tile_index = _cast_tile(rows, cols, gi, gj)
        ins.append(w)
        in_specs.append(pl.BlockSpec(
            (None,) + blk, lambda *g, idx=idx, t=tile_index: (idx,) + t(*pick(*g))))
        out_shape.append(jax.ShapeDtypeStruct((1, rows, cols), BF16))
        out_specs.append(pl.BlockSpec((None,) + blk, lambda *g, t=tile_index: (0,) + t(*pick(*g))))
    return ins, in_specs, out_shape, out_specs


def _run_casts(src_refs, dst_refs):
    for src, dst in zip(src_refs, dst_refs):
        dst[...] = src[...].astype(BF16)


def _rwkv_prep_kernel(*refs, tiles_per_seq, has_v, n_cast):
    (h_ref, hprev_ref, g_ref, mu_ref, w0_ref, w1_ref, w2_ref, a0_ref, a1_ref, a2_ref,
     g1_ref, g2_ref) = refs[:12]
    n_in = 15 if has_v else 12
    n_out = 5 if has_v else 4
    if has_v:
        v0_ref, v1_ref, v2_ref = refs[12:15]
    outs = refs[n_in + n_cast:n_in + n_cast + n_out]
    x3_ref, lw_ref, a_ref, gate_ref = outs[:4]
    _run_casts(refs[n_in:n_in + n_cast], refs[n_in + n_cast + n_out:])
    i = pl.program_id(0)
    g = g_ref[...]
    hn = _rms(h_ref[...], g)
    prev = _rms(hprev_ref[7:8, :], g)
    prev = jnp.where(i % tiles_per_seq == 0, 0.0, prev)
    rows = lax.broadcasted_iota(jnp.int32, hn.shape, 0)
    shifted = jnp.where(rows == 0, prev, pltpu.roll(hn, 1, axis=0))
    xx = shifted - hn
    x3_ref[0] = (hn + xx * mu_ref[0:1, :]).astype(BF16)
    x3_ref[1] = (hn + xx * mu_ref[2:3, :]).astype(BF16)
    xv = (hn + xx * mu_ref[3:4, :]).astype(BF16)
    x3_ref[2] = xv
    xw = hn + xx * mu_ref[1:2, :]
    z = w0_ref[...] + _dot(jnp.tanh(_dot(xw, w1_ref[...])), w2_ref[...])
    lw_ref[...] = -math.exp(-0.5) * _sigmoid(z)
    xa = hn + xx * mu_ref[4:5, :]
    a_ref[...] = _sigmoid(a0_ref[...] + _dot(_dot(xa, a1_ref[...]), a2_ref[...])).astype(BF16)
    xg = hn + xx * mu_ref[5:6, :]
    gate_ref[...] = _dot(_sigmoid(_dot(xg, g1_ref[...])), g2_ref[...]).astype(BF16)
    if has_v:
        outs[4][...] = _sigmoid(
            v0_ref[...] + _dot(_dot(xv, v1_ref[...]), v2_ref[...])).astype(BF16)


def _rwkv_prep(h, seq_len, g, mu, w0, w1, w2, a0, a1, a2, g1, g2, v_lora, cast_jobs=(), tm=256):
    n, d = h.shape
    has_v = v_lora is not None
    row = lambda v: v.reshape(1, d)
    full = lambda arr: pl.BlockSpec(arr.shape, lambda i: (0,) * arr.ndim)
    tile = pl.BlockSpec((tm, d), lambda i: (i, 0))
    ins = [h, h, row(g), mu, row(w0), w1, w2, row(a0), a1, a2, g1, g2]
    in_specs = [tile, pl.BlockSpec((8, d), lambda i: (jnp.maximum(i * (tm // 8) - 1, 0), 0))]
    in_specs += [full(x) for x in ins[2:]]
    out_shape = [jax.ShapeDtypeStruct((3, n, d), BF16), jax.ShapeDtypeStruct((n, d), F32),
                 jax.ShapeDtypeStruct((n, d), BF16), jax.ShapeDtypeStruct((n, d), BF16)]
    out_specs = [pl.BlockSpec((3, tm, d), lambda i: (0, i, 0)), tile, tile, tile]
    if has_v:
        v0, v1, v2 = v_lora
        extra = [row(v0), v1, v2]
        ins += extra
        in_specs += [full(x) for x in extra]
        out_shape.append(jax.ShapeDtypeStruct((n, d), BF16))
        out_specs.append(tile)
    c_ins, c_in_specs, c_out_shape, c_out_specs = _cast_specs(
        cast_jobs, n // tm, 1, lambda i: (i, 0))
    ins, in_specs = ins + c_ins, in_specs + c_in_specs
    out_shape, out_specs = out_shape + c_out_shape, out_specs + c_out_specs
    return pl.pallas_call(
        functools.partial(_rwkv_prep_kernel, tiles_per_seq=seq_len // tm, has_v=has_v,
                          n_cast=len(cast_jobs)),
        grid=(n // tm,), in_specs=in_specs, out_specs=out_specs, out_shape=out_shape,
        compiler_params=_cparams(("parallel",)))(*ins)


def _mm3_kernel(x_ref, w_ref, o_ref):
    o_ref[...] = jnp.dot(x_ref[...], w_ref[...], preferred_element_type=F32)


def _mm3(x3, w4, layer, tm=1024, tn=2048):
    p, n, k = x3.shape
    m = w4.shape[3]
    tm, tn = min(tm, n), min(tn, m)
    return pl.pallas_call(
        _mm3_kernel, grid=(p, n // tm, m // tn),
        in_specs=[pl.BlockSpec((None, tm, k), lambda q, i, j: (q, i, 0)),
                  pl.BlockSpec((None, None, k, tn), lambda q, i, j: (layer, q, 0, j))],
        out_specs=pl.BlockSpec((None, tm, tn), lambda q, i, j: (q, i, j)),
        out_shape=jax.ShapeDtypeStruct((p, n, m), F32),
        compiler_params=_cparams(("parallel", "parallel", "arbitrary")))(x3, w4)


def _mm_res_kernel(x_ref, w_ref, r_ref, o_ref):
    o_ref[...] = r_ref[...] + jnp.dot(x_ref[...], w_ref[...], preferred_element_type=F32)


def _mm_res(x, w3, layer, res, tm=512, tn=2048):
    n, k = x.shape
    m = w3.shape[2]
    tm, tn = min(tm, n), min(tn, m)
    return pl.pallas_call(
        _mm_res_kernel, grid=(n // tm, m // tn),
        in_specs=[pl.BlockSpec((tm, k), lambda i, j: (i, 0)),
                  pl.BlockSpec((None, k, tn), lambda i, j: (layer, 0, j)),
                  pl.BlockSpec((tm, tn), lambda i, j: (i, j))],
        out_specs=pl.BlockSpec((tm, tn), lambda i, j: (i, j)),
        out_shape=jax.ShapeDtypeStruct((n, m), F32),
        compiler_params=_cparams(("parallel", "arbitrary")))(x, w3, res)


def _norm_mm_rope_kernel(h_ref, g_ref, w_ref, cos_ref, sin_ref, o_ref, hn_ref, *, rope_cols):
    j = pl.program_id(1)

    @pl.when(j == 0)
    def _():
        hn_ref[...] = _rms(h_ref[...], g_ref[...]).astype(BF16)

    acc = jnp.dot(hn_ref[...], w_ref[...], preferred_element_type=F32)
    tn = acc.shape[1]
    full_tiles, part = divmod(rope_cols, tn)

    def store(rope_lanes):
        cos, sin = cos_ref[...], sin_ref[...]
        for hh in range(tn // ATT_HEAD):
            sl = slice(hh * ATT_HEAD, (hh + 1) * ATT_HEAD)
            x = acc[:, sl]
            if hh * ATT_HEAD < rope_lanes:
                x = x * cos + pltpu.roll(x, ATT_HEAD // 2, axis=1) * sin
            o_ref[:, sl] = x

    pl.when(j < full_tiles)(lambda: store(tn))
    if part:
        pl.when(j == full_tiles)(lambda: store(part))
    pl.when(j >= full_tiles + (1 if part else 0))(lambda: store(0))


def _norm_mm_rope(h, g, w3, layer, cos, sin, rope_cols, seq_len, tm=1024, tn=1024):
    n, d = h.shape
    m = w3.shape[2]
    tm, tn = min(tm, seq_len), min(tn, m)
    return pl.pallas_call(
        functools.partial(_norm_mm_rope_kernel, rope_cols=rope_cols),
        grid=(n // tm, m // tn),
        in_specs=[pl.BlockSpec((tm, d), lambda i, j: (i, 0)),
                  pl.BlockSpec((1, d), lambda i, j: (0, 0)),
                  pl.BlockSpec((None, d, tn), lambda i, j: (layer, 0, j)),
                  pl.BlockSpec((tm, ATT_HEAD), lambda i, j: (i % (seq_len // tm), 0)),
                  pl.BlockSpec((tm, ATT_HEAD), lambda i, j: (i % (seq_len // tm), 0))],
        out_specs=pl.BlockSpec((tm, tn), lambda i, j: (i, j)),
        out_shape=jax.ShapeDtypeStruct((n, m), F32),
        scratch_shapes=[pltpu.VMEM((tm, d), BF16)],
        compiler_params=_cparams(("parallel", "arbitrary")))(h, g.reshape(1, d), w3, cos, sin)


def _mlp_kernel(*refs, final_norm, n_cast):
    h_ref, g_ref, w1_ref, w2_ref, gf_ref = refs[:5]
    src_refs = refs[5:5 + n_cast]
    o_ref = refs[5 + n_cast]
    dst_refs = refs[6 + n_cast:6 + 2 * n_cast]
    hn_ref = refs[6 + 2 * n_cast]
    _run_casts(src_refs, dst_refs)
    j = pl.program_id(1)

    @pl.when(j == 0)
    def _():
        h = h_ref[...]
        hn_ref[...] = _rms(h, g_ref[...]).astype(BF16)
        o_ref[...] = h

    u = jnp.dot(hn_ref[...], w1_ref[...], preferred_element_type=F32)
    u = jnp.square(jnp.maximum(u, 0.0)).astype(BF16)
    o_ref[...] += jnp.dot(u, w2_ref[...], preferred_element_type=F32)

    if final_norm:
        @pl.when(j == pl.num_programs(1) - 1)
        def _():
            o_ref[...] = _rms(o_ref[...], gf_ref[...])


def _mlp(h, g, w1, w2, layer, gf, final_norm, cast_jobs=(), tm=1024, tf=512):
    n, d = h.shape
    f = w1.shape[2]
    tm = min(tm, n)
    gi, gj = n // tm, f // tf
    ins = [h, g.reshape(1, d), w1, w2, gf.reshape(1, d)]
    in_specs = [pl.BlockSpec((tm, d), lambda i, j: (i, 0)),
                pl.BlockSpec((1, d), lambda i, j: (0, 0)),
                pl.BlockSpec((None, d, tf), lambda i, j: (layer, 0, j)),
                pl.BlockSpec((None, tf, d), lambda i, j: (layer, j, 0)),
                pl.BlockSpec((1, d), lambda i, j: (0, 0))]
    out_shape = [jax.ShapeDtypeStruct((n, d), F32)]
    out_specs = [pl.BlockSpec((tm, d), lambda i, j: (i, 0))]
    c_ins, c_in_specs, c_out_shape, c_out_specs = _cast_specs(
        cast_jobs, gi, gj, lambda i, j: (i, j))
    ins, in_specs = ins + c_ins, in_specs + c_in_specs
    out_shape, out_specs = out_shape + c_out_shape, out_specs + c_out_specs
    return pl.pallas_call(
        functools.partial(_mlp_kernel, final_norm=final_norm, n_cast=len(cast_jobs)),
        grid=(gi, gj), in_specs=in_specs, out_specs=out_specs, out_shape=out_shape,
        scratch_shapes=[pltpu.VMEM((tm, d), BF16)],
        compiler_params=_cparams(("parallel", "arbitrary")))(*ins)


def _wkv_kernel(*refs, has_v, n_cast):
    r_ref, k_ref, v_ref, lw_ref, a_ref, gate_ref = refs[:6]
    nxt = 6
    if has_v:
        vf_ref, vg_ref = refs[6:8]
        nxt = 8
    kk_ref, ka_ref, rk_ref, gnw_ref, gnb_ref = refs[nxt:nxt + 5]
    o_ref = refs[nxt + 5 + n_cast]
    st_ref = refs[-1]
    _run_casts(refs[nxt + 5:nxt + 5 + n_cast], refs[nxt + 6 + n_cast:-1])

    @pl.when(pl.program_id(2) == 0)
    def _():
        st_ref[...] = jnp.zeros_like(st_ref)

    L = WKV_CHUNK
    n2 = 2 * L
    half = RWKV_HEAD
    m0 = lax.broadcasted_iota(jnp.int32, (L, LANES), 1) < half
    zeros = jnp.zeros((L, LANES), F32)
    tri = (lax.broadcasted_iota(jnp.int32, (L, L), 1)
           <= lax.broadcasted_iota(jnp.int32, (L, L), 0)).astype(BF16)

    def h0(x):
        return jnp.where(m0, x, 0.0)

    def h1(x):
        return jnp.where(m0, 0.0, x)

    def seg_sum(x):
        s0 = jnp.sum(h0(x), axis=-1, keepdims=True)
        s1 = jnp.sum(h1(x), axis=-1, keepdims=True)
        return jnp.where(m0, s0, s1)

    def vcat(*xs):
        return jnp.concatenate(xs, axis=0)

    row = lax.broadcasted_iota(jnp.int32, (n2, n2), 0)
    col = lax.broadcasted_iota(jnp.int32, (n2, n2), 1)
    keep = (col % L) < (row % L) + jnp.where(row < L, 0, 1)
    same_head = (row // L) == (col // L)
    steps = int(math.log2(L))

    def pair_stages(sub, p):
        sl = slice(p * LANES, (p + 1) * LANES)
        rs = slice(sub * L, (sub + 1) * L)
        r, k, v, lw = r_ref[rs, sl], k_ref[rs, sl], v_ref[rs, sl], lw_ref[rs, sl]
        a = a_ref[rs, sl].astype(F32)
        if has_v:
            v = v + (vf_ref[rs, sl] - v) * vg_ref[rs, sl].astype(F32)
        kq = k * kk_ref[:, sl]
        kk = kq * lax.rsqrt(jnp.maximum(seg_sum(kq * kq), 1e-24))
        k2 = k * (1.0 + (a - 1.0) * ka_ref[:, sl])
        b = kk * a
        lw_hi = lw.astype(BF16)
        lw_lo = (lw - lw_hi.astype(F32)).astype(BF16)
        c = (jnp.dot(tri, lw_hi, preferred_element_type=F32)
             + jnp.dot(tri, lw_lo, preferred_element_type=F32))
        c_last = c[L - 1:L, :]
        e_neg = jnp.exp(-c)
        e_last = jnp.exp(c_last - c)
        at, rt, bt, kt = -kk * jnp.exp(c - lw), r * jnp.exp(c), b * e_neg, k2 * e_neg
        vr = pltpu.roll(v, half, axis=1)
        bk = vcat(b * e_last, k2 * e_last)
        yield
        a0 = jnp.where(keep, _dot_nt(vcat(h0(at), h0(rt)), vcat(bt, kt)), 0.0)
        a1 = jnp.where(keep, _dot_nt(vcat(h1(at), h1(rt)), vcat(kt, bt)), 0.0)
        yield
        x = vcat(h0(at) + _dot(a0[:L], vcat(zeros, h1(vr))),
                 h1(at) + _dot(a1[:L], vcat(h0(vr), zeros)))
        pcat = jnp.where(m0, a0[:L], a1[:L])
        yield
        for s in range(steps):
            pbd = vcat(h0(pcat), h1(pcat))
            x = x + _dot(pbd, x)
            if s + 1 < steps:
                pcat = _dot(pcat, pbd)
            yield
        wcat = jnp.where(m0, x[:L], x[L:])
        u0 = pltpu.roll(jnp.where(m0, x[L:], x[:L]), half, axis=1)
        wh = _dot_nt(vcat(wcat, rt), st_ref[p])
        yield
        u = wh[:L] + u0
        y = wh[L:] + _dot(jnp.concatenate([a0[L:], a1[L:]], axis=1),
                          vcat(h0(u), h0(v), h1(v), h1(u)))
        st_ref[p] = st_ref[p] * jnp.exp(c_last) + jnp.where(
            same_head, _dot_tn(vcat(u, v), bk), 0.0)
        yield
        mean = seg_sum(y) * (1.0 / half)
        dlt = y - mean
        var = seg_sum(dlt * dlt) * (1.0 / half)
        yn = dlt * lax.rsqrt(var + GN_EPS) * gnw_ref[:, sl] + gnb_ref[:, sl]
        bonus = seg_sum(r * k2 * rk_ref[:, sl]) * v
        o_ref[rs, sl] = ((yn + bonus) * gate_ref[rs, sl].astype(F32)).astype(BF16)

    n_pairs = o_ref.shape[1] // LANES
    n_sub = o_ref.shape[0] // L
    plan = {}
    for sub in range(n_sub):
        t0 = sub * WKV_CHAIN
        for p in range(n_pairs):
            spread = 1 + (p * WKV_CHAIN) // n_pairs
            first = t0 if sub == 0 else t0 - WKV_CHAIN + spread
            last = t0 + WKV_CHAIN + (1 if sub == n_sub - 1 else spread)
            gen = pair_stages(sub, p)
            for order, tick in enumerate([first] + [t0 + s for s in range(1, WKV_CHAIN + 1)] + [last]):
                plan.setdefault(tick, []).append((order in (0, WKV_CHAIN + 1), sub, p, gen))
    for tick in sorted(plan):
        for _, _, _, gen in sorted(plan[tick], key=lambda e: e[:3]):
            next(gen, None)


def _wkv(rkv, lw, a, gate, v_first, v_gate, k_k, k_a, r_k, gn_w, gn_b, batch, cast_jobs=(),
         width=2048):
    _, n, d = rkv.shape
    width = min(width, d)
    assert not cast_jobs or width == d
    L = WKV_CHUNK * WKV_SUBCHUNKS
    nc = n // batch // L
    has_v = v_first is not None
    tile = pl.BlockSpec((L, width), lambda b, p, c: (b * nc + c, p))
    rkv_spec = lambda q: pl.BlockSpec((None, L, width), lambda b, p, c: (q, b * nc + c, p))
    prm = pl.BlockSpec((1, width), lambda b, p, c: (0, p))
    ins = [rkv, rkv, rkv, lw, a, gate]
    in_specs = [rkv_spec(0), rkv_spec(1), rkv_spec(2), tile, tile, tile]
    if has_v:
        ins += [v_first, v_gate]
        in_specs += [rkv_spec(2), tile]
    ins += [x.reshape(1, d) for x in (k_k, k_a, r_k, gn_w, gn_b)]
    in_specs += [prm] * 5
    c_ins, c_in_specs, c_out_shape, c_out_specs = _cast_specs(
        cast_jobs, batch, nc, lambda b, p, c: (b, c))
    return pl.pallas_call(
        functools.partial(_wkv_kernel, has_v=has_v, n_cast=len(cast_jobs)),
        grid=(batch, d // width, nc), in_specs=in_specs + c_in_specs,
        out_specs=[tile] + c_out_specs,
        out_shape=[jax.ShapeDtypeStruct((n, d), BF16)] + c_out_shape,
        scratch_shapes=[pltpu.VMEM((width // LANES, LANES, LANES), F32)],
        compiler_params=_cparams(("parallel", "parallel", "arbitrary")))(*ins, *c_ins)


def _moba_kernel(q_ref, k_ref, v_ref, o_ref):
    bs = MOBA_BLOCK
    t = k_ref.shape[0]
    nb = t // bs
    nbp = 16
    qscale = (ATT_HEAD ** -0.5) * math.log2(math.e)
    kf = k_ref[...]
    kmean = jnp.mean(kf.reshape(nb, bs, ATT_HEAD), axis=1)
    kmean = jnp.concatenate([kmean, jnp.zeros((nbp - nb, ATT_HEAD), F32)], axis=0)
    km1, km2, _ = _split3(kmean)
    key_blk = lax.broadcasted_iota(jnp.int32, (t, LANES), 0) // bs
    key_lane = lax.broadcasted_iota(jnp.int32, (t, LANES), 1)
    k_aug = jnp.concatenate(
        [kf.astype(BF16), jnp.where(key_blk == key_lane, MASKED_LOGIT, 0.0).astype(BF16)], axis=1)
    v_aug = jnp.concatenate(
        [v_ref[...].astype(BF16), jnp.where(key_lane == 0, 1.0, 0.0).astype(BF16)], axis=1)
    eye = (lax.broadcasted_iota(jnp.int32, (nbp, LANES), 0)
           == lax.broadcasted_iota(jnp.int32, (nbp, LANES), 1)).astype(BF16)
    group = q_ref.shape[1] // ATT_HEAD
    rows = group * bs
    blk = lax.broadcasted_iota(jnp.int32, (nbp, rows), 0)
    qi = lax.broadcasted_iota(jnp.int32, (rows, bs), 0) % bs
    ki = lax.broadcasted_iota(jnp.int32, (rows, bs), 1)
    causal_bias = jnp.where(ki <= qi, 0.0, MASKED_LOGIT)
    for qb in range(nb):
        own = slice(qb * bs, (qb + 1) * bs)
        q = jnp.concatenate([q_ref[own, g * ATT_HEAD:(g + 1) * ATT_HEAD] for g in range(group)],
                            axis=0)
        qs = (q * qscale).astype(BF16)
        l_own = _dot_nt(qs, k_aug[own, :ATT_HEAD]) + causal_bias
        m = jnp.max(l_own, axis=-1, keepdims=True)
        if qb > 0:
            if qb > MOBA_TOPK:
                q1, q2, _ = _split3(q)
                gate = _dot_nt(km1, q1) + _dot_nt(km1, q2) + _dot_nt(km2, q1)
                cnt = jnp.zeros((nbp, rows), jnp.int32)
                for mb in range(qb):
                    gm = gate[mb:mb + 1, :]
                    beats = (gm > gate) | ((gm == gate) & (mb < blk))
                    cnt = cnt + beats.astype(jnp.int32)
                notsel = jnp.where((cnt >= MOBA_TOPK) & (blk < qb), 1.0, 0.0)
                flags = _dot_tn(notsel, eye).astype(BF16)
            else:
                flags = jnp.zeros((rows, LANES), BF16)
            l_past = _dot_nt(jnp.concatenate([qs, flags], axis=1), k_aug[:qb * bs])
            m = jnp.maximum(m, jnp.max(l_past, axis=-1, keepdims=True))
            acc = (_dot(jnp.exp2(l_past - m), v_aug[:qb * bs]) + _dot(jnp.exp2(l_own - m), v_aug[own]))
        else:
            acc = _dot(jnp.exp2(l_own - m), v_aug[own])
        out = (acc[:, :ATT_HEAD] / acc[:, ATT_HEAD:ATT_HEAD + 1]).astype(BF16)
        for g in range(group):
            o_ref[own, g * ATT_HEAD:(g + 1) * ATT_HEAD] = out[g * bs:(g + 1) * bs]


def _moba(q, kv, batch):
    n, dq = q.shape
    t = n // batch
    gw = dq // KV_HEADS
    return pl.pallas_call(
        _moba_kernel, grid=(batch, KV_HEADS),
        in_specs=[pl.BlockSpec((t, gw), lambda b, kh: (b, kh)),
                  pl.BlockSpec((t, ATT_HEAD), lambda b, kh: (b, kh)),
                  pl.BlockSpec((t, ATT_HEAD), lambda b, kh: (b, KV_HEADS + kh))],
        out_specs=pl.BlockSpec((t, gw), lambda b, kh: (b, kh)),
        out_shape=jax.ShapeDtypeStruct((n, dq), BF16),
        compiler_params=_cparams(("parallel", "parallel")))(q, kv, kv)


def _pad_lora(w_in, w_out):
    r = w_in.shape[1]
    rp = -(-r // LANES) * LANES
    return (jnp.pad(w_in, ((0, 0), (0, rp - r))).astype(BF16),
            jnp.pad(w_out, ((0, rp - r), (0, 0))).astype(BF16))


def _rope_tables(t):
    half = ATT_HEAD // 2
    inv = ROPE_THETA ** (-jnp.arange(half, dtype=F32) / half)
    ang = jnp.arange(t, dtype=F32)[:, None] * inv[None, :]
    cos, sin = jnp.cos(ang), jnp.sin(ang)
    return jnp.concatenate([cos, cos], axis=-1), jnp.concatenate([-sin, sin], axis=-1)


def kernel(x, ln_mix_g, ln_ffn_g, w_ff1, w_ff2, rw_mu, rw_w_rkv, rw_w0, rw_w1, rw_w2, rw_a0, rw_a1, rw_a2, rw_g1, rw_g2, rw_k_k, rw_k_a, rw_r_k, rw_gn_w, rw_gn_b, rw_w_o, rw_v0, rw_v1, rw_v2, kv_norm_g, w_kv, mb_w_q, mb_w_o, final_g):
    batch, t, d = x.shape
    n = batch * t
    depth = ln_mix_g.shape[0]
    n_rwkv = rw_mu.shape[0]
    h = x.reshape(n, d)
    cos, sin = _rope_tables(t)
    rkv_f32 = rw_w_rkv.reshape(n_rwkv, 3 * d, d)
    v_first = None
    kv = None
    for layer in range(depth):
        if layer < n_rwkv:
            i = layer
            w1, w2 = _pad_lora(rw_w1[i], rw_w2[i])
            a1, a2 = _pad_lora(rw_a1[i], rw_a2[i])
            g1, g2 = _pad_lora(rw_g1[i], rw_g2[i])
            v_lora = None
            if i > 0:
                v1, v2 = _pad_lora(rw_v1[i - 1], rw_v2[i - 1])
                v_lora = (rw_v0[i - 1], v1, v2)
            first = layer == 0
            prep = _rwkv_prep(h, t, ln_mix_g[layer], rw_mu[i], rw_w0[i], w1, w2,
                              rw_a0[i], a1, a2, g1, g2, v_lora,
                              [(rkv_f32, 0)] if first else [])
            x3, lw, a, gate = prep[:4]
            v_gate = prep[4] if i > 0 else None
            if first:
                rkv_b = prep[-1]
            rkv = _mm3(x3, rkv_b.reshape(1, 3, d, d), 0)
            res = _wkv(rkv, lw, a, gate, v_first, v_gate, rw_k_k[i], rw_k_a[i], rw_r_k[i],
                       rw_gn_w[i], rw_gn_b[i], batch,
                       [(w_ff1, 0), (w_ff2, 0), (rw_w_o, 0)] if first else [])
            yg = res[0]
            if first:
                w1_b, w2_b, wo_b = res[1:]
            if i == 0:
                v_first = rkv
            h = _mm_res(yg, wo_b, 0, h)
        else:
            if kv is None:
                kv = _norm_mm_rope(h, kv_norm_g, wkv_b, 0, cos, sin, KV_HEADS * ATT_HEAD, t)
            q = _norm_mm_rope(h, ln_mix_g[layer], wq_b, 0, cos, sin, mb_w_q.shape[2], t)
            att = _moba(q, kv, batch)
            h = _mm_res(att, wo_b, 0, h)
        nxt = layer + 1
        jobs = []
        if nxt < depth:
            jobs = [(w_ff1, nxt), (w_ff2, nxt)]
            if nxt < n_rwkv:
                jobs += [(rkv_f32, nxt), (rw_w_o, nxt)]
            else:
                jobs += [(mb_w_q, nxt - n_rwkv), (mb_w_o, nxt - n_rwkv)]
                if nxt == n_rwkv:
                    jobs.append((w_kv[None], 0))
        res = _mlp(h, ln_ffn_g[layer], w1_b, w2_b, 0, final_g, nxt == depth, jobs)
        h = res[0]
        if jobs:
            w1_b, w2_b = res[1:3]
            if nxt < n_rwkv:
                rkv_b, wo_b = res[3:5]
            else:
                wq_b, wo_b = res[3:5]
                if nxt == n_rwkv:
                    wkv_b = res[5]
    return h.reshape(batch, t, d)
```

```python
import functools
import math

import jax
import jax.numpy as jnp
from jax import lax
from jax.experimental import pallas as pl
from jax.experimental.pallas import tpu as pltpu

F32 = jnp.float32
BF16 = jnp.bfloat16

RMS_EPS = 1e-6
GN_EPS = 64e-5
RWKV_HEAD = 64
ATT_HEAD = 128
KV_HEADS = 4
MOBA_BLOCK = 256
MOBA_TOPK = 3
ROPE_THETA = 10000.0
LANES = 128
WKV_CHUNK = 64
WKV_CHAIN = 10
WKV_SUBCHUNKS = 4
MASKED_LOGIT = -1e30
VMEM_LIMIT = 56 * 1024 * 1024


def _cparams(sem):
    return pltpu.CompilerParams(dimension_semantics=sem, vmem_limit_bytes=VMEM_LIMIT)


def _rms(x, g):
    return x * lax.rsqrt(jnp.mean(x * x, axis=-1, keepdims=True) + RMS_EPS) * g


def _sigmoid(x):
    return 0.5 * jnp.tanh(0.5 * x) + 0.5


def _dot(a, b):
    return jnp.dot(a.astype(BF16), b.astype(BF16), preferred_element_type=F32)


def _dot_nt(a, b):
    return lax.dot_general(a.astype(BF16), b.astype(BF16), (((1,), (1,)), ((), ())),
                           preferred_element_type=F32)


def _dot_tn(a, b):
    return lax.dot_general(a.astype(BF16), b.astype(BF16), (((0,), (0,)), ((), ())),
                           preferred_element_type=F32)


def _split3(x):
    h1 = x.astype(BF16)
    r1 = x - h1.astype(F32)
    h2 = r1.astype(BF16)
    h3 = (r1 - h2.astype(F32)).astype(BF16)
    return h1, h2, h3


def _cast_tile(rows, cols, gi, gj):
    if (rows // gi) % 16 == 0 and (cols // gj) % LANES == 0:
        return (rows // gi, cols // gj), lambda i, j: (i, j)
    assert (rows // gj) % 16 == 0 and (cols // gi) % LANES == 0, (rows, cols, gi, gj)
    return (rows // gj, cols // gi), lambda i, j: (j, i)


def _cast_specs(cast_jobs, gi, gj, pick):
    ins, in_specs, out_shape, out_specs = [], [], [], []
    for w, idx in cast_jobs:
        _, rows, cols = w.shape
        blk, tile_index = _cast_tile(rows, cols, gi, gj)
        ins.append(w)
        in_specs.append(pl.BlockSpec(
            (None,) + blk, lambda *g, idx=idx, t=tile_index: (idx,) + t(*pick(*g))))
        out_shape.append(jax.ShapeDtypeStruct((1, rows, cols), BF16))
        out_specs.append(pl.BlockSpec((None,) + blk, lambda *g, t=tile_index: (0,) + t(*pick(*g))))
    return ins, in_specs, out_shape, out_specs


def _run_casts(src_refs, dst_refs):
    for src, dst in zip(src_refs, dst_refs):
        dst[...] = src[...].astype(BF16)


def _rwkv_prep_kernel(*refs, tiles_per_seq, has_v, n_cast):
    (h_ref, hprev_ref, g_ref, mu_ref, w0_ref, w1_ref, w2_ref, a0_ref, a1_ref, a2_ref,
     g1_ref, g2_ref) = refs[:12]
    n_in = 15 if has_v else 12
    n_out = 5 if has_v else 4
    if has_v:
        v0_ref, v1_ref, v2_ref = refs[12:15]
    outs = refs[n_in + n_cast:n_in + n_cast + n_out]
    x3_ref, lw_ref, a_ref, gate_ref = outs[:4]
    _run_casts(refs[n_in:n_in + n_cast], refs[n_in + n_cast + n_out:])
    i = pl.program_id(0)
    g = g_ref[...]
    hn = _rms(h_ref[...], g)
    prev = _rms(hprev_ref[7:8, :], g)
    prev = jnp.where(i % tiles_per_seq == 0, 0.0, prev)
    rows = lax.broadcasted_iota(jnp.int32, hn.shape, 0)
    shifted = jnp.where(rows == 0, prev, pltpu.roll(hn, 1, axis=0))
    xx = shifted - hn
    x3_ref[0] = (hn + xx * mu_ref[0:1, :]).astype(BF16)
    x3_ref[1] = (hn + xx * mu_ref[2:3, :]).astype(BF16)
    xv = (hn + xx * mu_ref[3:4, :]).astype(BF16)
    x3_ref[2] = xv
    xw = hn + xx * mu_ref[1:2, :]
    z = w0_ref[...] + _dot(jnp.tanh(_dot(xw, w1_ref[...])), w2_ref[...])
    lw_ref[...] = -math.exp(-0.5) * _sigmoid(z)
    xa = hn + xx * mu_ref[4:5, :]
    a_ref[...] = _sigmoid(a0_ref[...] + _dot(_dot(xa, a1_ref[...]), a2_ref[...])).astype(BF16)
    xg = hn + xx * mu_ref[5:6, :]
    gate_ref[...] = _dot(_sigmoid(_dot(xg, g1_ref[...])), g2_ref[...]).astype(BF16)
    if has_v:
        outs[4][...] = _sigmoid(
            v0_ref[...] + _dot(_dot(xv, v1_ref[...]), v2_ref[...])).astype(BF16)


def _rwkv_prep(h, seq_len, g, mu, w0, w1, w2, a0, a1, a2, g1, g2, v_lora, cast_jobs=(), tm=256):
    n, d = h.shape
    has_v = v_lora is not None
    row = lambda v: v.reshape(1, d)
    full = lambda arr: pl.BlockSpec(arr.shape, lambda i: (0,) * arr.ndim)
    tile = pl.BlockSpec((tm, d), lambda i: (i, 0))
    ins = [h, h, row(g), mu, row(w0), w1, w2, row(a0), a1, a2, g1, g2]
    in_specs = [tile, pl.BlockSpec((8, d), lambda i: (jnp.maximum(i * (tm // 8) - 1, 0), 0))]
    in_specs += [full(x) for x in ins[2:]]
    out_shape = [jax.ShapeDtypeStruct((3, n, d), BF16), jax.ShapeDtypeStruct((n, d), F32),
                 jax.ShapeDtypeStruct((n, d), BF16), jax.ShapeDtypeStruct((n, d), BF16)]
    out_specs = [pl.BlockSpec((3, tm, d), lambda i: (0, i, 0)), tile, tile, tile]
    if has_v:
        v0, v1, v2 = v_lora
        extra = [row(v0), v1, v2]
        ins += extra
        in_specs += [full(x) for x in extra]
        out_shape.append(jax.ShapeDtypeStruct((n, d), BF16))
        out_specs.append(tile)
    c_ins, c_in_specs, c_out_shape, c_out_specs = _cast_specs(
        cast_jobs, n // tm, 1, lambda i: (i, 0))
    ins, in_specs = ins + c_ins, in_specs + c_in_specs
    out_shape, out_specs = out_shape + c_out_shape, out_specs + c_out_specs
    return pl.pallas_call(
        functools.partial(_rwkv_prep_kernel, tiles_per_seq=seq_len // tm, has_v=has_v,
                          n_cast=len(cast_jobs)),
        grid=(n // tm,), in_specs=in_specs, out_specs=out_specs, out_shape=out_shape,
        compiler_params=_cparams(("parallel",)))(*ins)


def _mm3_kernel(x_ref, w_ref, o_ref):
    o_ref[...] = jnp.dot(x_ref[...], w_ref[...], preferred_element_type=F32)


def _mm3(x3, w4, layer, tm=1024, tn=2048):
    p, n, k = x3.shape
    m = w4.shape[3]
    tm, tn = min(tm, n), min(tn, m)
    return pl.pallas_call(
        _mm3_kernel, grid=(p, n // tm, m // tn),
        in_specs=[pl.BlockSpec((None, tm, k), lambda q, i, j: (q, i, 0)),
                  pl.BlockSpec((None, None, k, tn), lambda q, i, j: (layer, q, 0, j))],
        out_specs=pl.BlockSpec((None, tm, tn), lambda q, i, j: (q, i, j)),
        out_shape=jax.ShapeDtypeStruct((p, n, m), F32),
        compiler_params=_cparams(("parallel", "parallel", "arbitrary")))(x3, w4)


def _mm_res_kernel(x_ref, w_ref, r_ref, o_ref):
    o_ref[...] = r_ref[...] + jnp.dot(x_ref[...], w_ref[...], preferred_element_type=F32)


def _mm_res(x, w3, layer, res, tm=512, tn=2048):
    n, k = x.shape
    m = w3.shape[2]
    tm, tn = min(tm, n), min(tn, m)
    return pl.pallas_call(
        _mm_res_kernel, grid=(n // tm, m // tn),
        in_specs=[pl.BlockSpec((tm, k), lambda i, j: (i, 0)),
                  pl.BlockSpec((None, k, tn), lambda i, j: (layer, 0, j)),
                  pl.BlockSpec((tm, tn), lambda i, j: (i, j))],
        out_specs=pl.BlockSpec((tm, tn), lambda i, j: (i, j)),
        out_shape=jax.ShapeDtypeStruct((n, m), F32),
        compiler_params=_cparams(("parallel", "arbitrary")))(x, w3, res)


def _norm_mm_rope_kernel(h_ref, g_ref, w_ref, cos_ref, sin_ref, o_ref, hn_ref, *, rope_cols):
    j = pl.program_id(1)

    @pl.when(j == 0)
    def _():
        hn_ref[...] = _rms(h_ref[...], g_ref[...]).astype(BF16)

    acc = jnp.dot(hn_ref[...], w_ref[...], preferred_element_type=F32)
    tn = acc.shape[1]
    full_tiles, part = divmod(rope_cols, tn)

    def store(rope_lanes):
        cos, sin = cos_ref[...], sin_ref[...]
        for hh in range(tn // ATT_HEAD):
            sl = slice(hh * ATT_HEAD, (hh + 1) * ATT_HEAD)
            x = acc[:, sl]
            if hh * ATT_HEAD < rope_lanes:
                x = x * cos + pltpu.roll(x, ATT_HEAD // 2, axis=1) * sin
            o_ref[:, sl] = x

    pl.when(j < full_tiles)(lambda: store(tn))
    if part:
        pl.when(j == full_tiles)(lambda: store(part))
    pl.when(j >= full_tiles + (1 if part else 0))(lambda: store(0))


def _norm_mm_rope(h, g, w3, layer, cos, sin, rope_cols, seq_len, tm=1024, tn=1024):
    n, d = h.shape
    m = w3.shape[2]
    tm, tn = min(tm, seq_len), min(tn, m)
    return pl.pallas_call(
        functools.partial(_norm_mm_rope_kernel, rope_cols=rope_cols),
        grid=(n // tm, m // tn),
        in_specs=[pl.BlockSpec((tm, d), lambda i, j: (i, 0)),
                  pl.BlockSpec((1, d), lambda i, j: (0, 0)),
                  pl.BlockSpec((None, d, tn), lambda i, j: (layer, 0, j)),
                  pl.BlockSpec((tm, ATT_HEAD), lambda i, j: (i % (seq_len // tm), 0)),
                  pl.BlockSpec((tm, ATT_HEAD), lambda i, j: (i % (seq_len // tm), 0))],
        out_specs=pl.BlockSpec((tm, tn), lambda i, j: (i, j)),
        out_shape=jax.ShapeDtypeStruct((n, m), F32),
        scratch_shapes=[pltpu.VMEM((tm, d), BF16)],
        compiler_params=_cparams(("parallel", "arbitrary")))(h, g.reshape(1, d), w3, cos, sin)


def _mlp_kernel(*refs, final_norm, n_cast):
    h_ref, g_ref, w1_ref, w2_ref, gf_ref = refs[:5]
    src_refs = refs[5:5 + n_cast]
    o_ref = refs[5 + n_cast]
    dst_refs = refs[6 + n_cast:6 + 2 * n_cast]
    hn_ref = refs[6 + 2 * n_cast]
    _run_casts(src_refs, dst_refs)
    j = pl.program_id(1)

    @pl.when(j == 0)
    def _():
        h = h_ref[...]
        hn_ref[...] = _rms(h, g_ref[...]).astype(BF16)
        o_ref[...] = h

    u = jnp.dot(hn_ref[...], w1_ref[...], preferred_element_type=F32)
    u = jnp.square(jnp.maximum(u, 0.0)).astype(BF16)
    o_ref[...] += jnp.dot(u, w2_ref[...], preferred_element_type=F32)

    if final_norm:
        @pl.when(j == pl.num_programs(1) - 1)
        def _():
            o_ref[...] = _rms(o_ref[...], gf_ref[...])


def _mlp(h, g, w1, w2, layer, gf, final_norm, cast_jobs=(), tm=1024, tf=512):
    n, d = h.shape
    f = w1.shape[2]
    tm = min(tm, n)
    gi, gj = n // tm, f // tf
    ins = [h, g.reshape(1, d), w1, w2, gf.reshape(1, d)]
    in_specs = [pl.BlockSpec((tm, d), lambda i, j: (i, 0)),
                pl.BlockSpec((1, d), lambda i, j: (0, 0)),
                pl.BlockSpec((None, d, tf), lambda i, j: (layer, 0, j)),
                pl.BlockSpec((None, tf, d), lambda i, j: (layer, j, 0)),
                pl.BlockSpec((1, d), lambda i, j: (0, 0))]
    out_shape = [jax.ShapeDtypeStruct((n, d), F32)]
    out_specs = [pl.BlockSpec((tm, d), lambda i, j: (i, 0))]
    c_ins, c_in_specs, c_out_shape, c_out_specs = _cast_specs(
        cast_jobs, gi, gj, lambda i, j: (i, j))
    ins, in_specs = ins + c_ins, in_specs + c_in_specs
    out_shape, out_specs = out_shape + c_out_shape, out_specs + c_out_specs
    return pl.pallas_call(
        functools.partial(_mlp_kernel, final_norm=final_norm, n_cast=len(cast_jobs)),
        grid=(gi, gj), in_specs=in_specs, out_specs=out_specs, out_shape=out_shape,
        scratch_shapes=[pltpu.VMEM((tm, d), BF16)],
        compiler_params=_cparams(("parallel", "arbitrary")))(*ins)


def _wkv_kernel(*refs, has_v, n_cast):
    r_ref, k_ref, v_ref, lw_ref, a_ref, gate_ref = refs[:6]
    nxt = 6
    if has_v:
        vf_ref, vg_ref = refs[6:8]
        nxt = 8
    kk_ref, ka_ref, rk_ref, gnw_ref, gnb_ref = refs[nxt:nxt + 5]
    o_ref = refs[nxt + 5 + n_cast]
    st_ref = refs[-1]
    _run_casts(refs[nxt + 5:nxt + 5 + n_cast], refs[nxt + 6 + n_cast:-1])

    @pl.when(pl.program_id(2) == 0)
    def _():
        st_ref[...] = jnp.zeros_like(st_ref)

    L = WKV_CHUNK
    n2 = 2 * L
    half = RWKV_HEAD
    m0 = lax.broadcasted_iota(jnp.int32, (L, LANES), 1) < half
    zeros = jnp.zeros((L, LANES), F32)
    tri = (lax.broadcasted_iota(jnp.int32, (L, L), 1)
           <= lax.broadcasted_iota(jnp.int32, (L, L), 0)).astype(BF16)

    def h0(x):
        return jnp.where(m0, x, 0.0)

    def h1(x):
        return jnp.where(m0, 0.0, x)

    def seg_sum(x):
        s0 = jnp.sum(h0(x), axis=-1, keepdims=True)
        s1 = jnp.sum(h1(x), axis=-1, keepdims=True)
        return jnp.where(m0, s0, s1)

    def vcat(*xs):
        return jnp.concatenate(xs, axis=0)

    row = lax.broadcasted_iota(jnp.int32, (n2, n2), 0)
    col = lax.broadcasted_iota(jnp.int32, (n2, n2), 1)
    keep = (col % L) < (row % L) + jnp.where(row < L, 0, 1)
    same_head = (row // L) == (col // L)
    steps = int(math.log2(L))

    def pair_stages(sub, p):
        sl = slice(p * LANES, (p + 1) * LANES)
        rs = slice(sub * L, (sub + 1) * L)
        r, k, v, lw = r_ref[rs, sl], k_ref[rs, sl], v_ref[rs, sl], lw_ref[rs, sl]
        a = a_ref[rs, sl].astype(F32)
        if has_v:
            v = v + (vf_ref[rs, sl] - v) * vg_ref[rs, sl].astype(F32)
        kq = k * kk_ref[:, sl]
        kk = kq * lax.rsqrt(jnp.maximum(seg_sum(kq * kq), 1e-24))
        k2 = k * (1.0 + (a - 1.0) * ka_ref[:, sl])
        b = kk * a
        lw_hi = lw.astype(BF16)
        lw_lo = (lw - lw_hi.astype(F32)).astype(BF16)
        c = (jnp.dot(tri, lw_hi, preferred_element_type=F32)
             + jnp.dot(tri, lw_lo, preferred_element_type=F32))
        c_last = c[L - 1:L, :]
        e_neg = jnp.exp(-c)
        e_last = jnp.exp(c_last - c)
        at, rt, bt, kt = -kk * jnp.exp(c - lw), r * jnp.exp(c), b * e_neg, k2 * e_neg
        vr = pltpu.roll(v, half, axis=1)
        bk = vcat(b * e_last, k2 * e_last)
        yield
        a0 = jnp.where(keep, _dot_nt(vcat(h0(at), h0(rt)), vcat(bt, kt)), 0.0)
        a1 = jnp.where(keep, _dot_nt(vcat(h1(at), h1(rt)), vcat(kt, bt)), 0.0)
        yield
        x = vcat(h0(at) + _dot(a0[:L], vcat(zeros, h1(vr))),
                 h1(at) + _dot(a1[:L], vcat(h0(vr), zeros)))
        pcat = jnp.where(m0, a0[:L], a1[:L])
        yield
        for s in range(steps):
            pbd = vcat(h0(pcat), h1(pcat))
            x = x + _dot(pbd, x)
            if s + 1 < steps:
                pcat = _dot(pcat, pbd)
            yield
        wcat = jnp.where(m0, x[:L], x[L:])
        u0 = pltpu.roll(jnp.where(m0, x[L:], x[:L]), half, axis=1)
        wh = _dot_nt(vcat(wcat, rt), st_ref[p])
        yield
        u = wh[:L] + u0
        y = wh[L:] + _dot(jnp.concatenate([a0[L:], a1[L:]], axis=1),
                          vcat(h0(u), h0(v), h1(v), h1(u)))
        st_ref[p] = st_ref[p] * jnp.exp(c_last) + jnp.where(
            same_head, _dot_tn(vcat(u, v), bk), 0.0)
        yield
        mean = seg_sum(y) * (1.0 / half)
        dlt = y - mean
        var = seg_sum(dlt * dlt) * (1.0 / half)
        yn = dlt * lax.rsqrt(var + GN_EPS) * gnw_ref[:, sl] + gnb_ref[:, sl]
        bonus = seg_sum(r * k2 * rk_ref[:, sl]) * v
        o_ref[rs, sl] = ((yn + bonus) * gate_ref[rs, sl].astype(F32)).astype(BF16)

    n_pairs = o_ref.shape[1] // LANES
    n_sub = o_ref.shape[0] // L
    plan = {}
    for sub in range(n_sub):
        t0 = sub * WKV_CHAIN
        for p in range(n_pairs):
            spread = 1 + (p * WKV_CHAIN) // n_pairs
            first = t0 if sub == 0 else t0 - WKV_CHAIN + spread
            last = t0 + WKV_CHAIN + (1 if sub == n_sub - 1 else spread)
            gen = pair_stages(sub, p)
            for order, tick in enumerate([first] + [t0 + s for s in range(1, WKV_CHAIN + 1)] + [last]):
                plan.setdefault(tick, []).append((order in (0, WKV_CHAIN + 1), sub, p, gen))
    for tick in sorted(plan):
        for _, _, _, gen in sorted(plan[tick], key=lambda e: e[:3]):
            next(gen, None)


def _wkv(rkv, lw, a, gate, v_first, v_gate, k_k, k_a, r_k, gn_w, gn_b, batch, cast_jobs=(),
         width=2048):
    _, n, d = rkv.shape
    width = min(width, d)
    assert not cast_jobs or width == d
    L = WKV_CHUNK * WKV_SUBCHUNKS
    nc = n // batch // L
    has_v = v_first is not None
    tile = pl.BlockSpec((L, width), lambda b, p, c: (b * nc + c, p))
    rkv_spec = lambda q: pl.BlockSpec((None, L, width), lambda b, p, c: (q, b * nc + c, p))
    prm = pl.BlockSpec((1, width), lambda b, p, c: (0, p))
    ins = [rkv, rkv, rkv, lw, a, gate]
    in_specs = [rkv_spec(0), rkv_spec(1), rkv_spec(2), tile, tile, tile]
    if has_v:
        ins += [v_first, v_gate]
        in_specs += [rkv_spec(2), tile]
    ins += [x.reshape(1, d) for x in (k_k, k_a, r_k, gn_w, gn_b)]
    in_specs += [prm] * 5
    c_ins, c_in_specs, c_out_shape, c_out_specs = _cast_specs(
        cast_jobs, batch, nc, lambda b, p, c: (b, c))
    return pl.pallas_call(
        functools.partial(_wkv_kernel, has_v=has_v, n_cast=len(cast_jobs)),
        grid=(batch, d // width, nc), in_specs=in_specs + c_in_specs,
        out_specs=[tile] + c_out_specs,
        out_shape=[jax.ShapeDtypeStruct((n, d), BF16)] + c_out_shape,
        scratch_shapes=[pltpu.VMEM((width // LANES, LANES, LANES), F32)],
        compiler_params=_cparams(("parallel", "parallel", "arbitrary")))(*ins, *c_ins)


def _moba_kernel(q_ref, k_ref, v_ref, o_ref):
    bs = MOBA_BLOCK
    t = k_ref.shape[0]
    nb = t // bs
    nbp = 16
    qscale = (ATT_HEAD ** -0.5) * math.log2(math.e)
    kf = k_ref[...]
    kmean = jnp.mean(kf.reshape(nb, bs, ATT_HEAD), axis=1)
    kmean = jnp.concatenate([kmean, jnp.zeros((nbp - nb, ATT_HEAD), F32)], axis=0)
    km1, km2, _ = _split3(kmean)
    key_blk = lax.broadcasted_iota(jnp.int32, (t, LANES), 0) // bs
    key_lane = lax.broadcasted_iota(jnp.int32, (t, LANES), 1)
    k_aug = jnp.concatenate(
        [kf.astype(BF16), jnp.where(key_blk == key_lane, MASKED_LOGIT, 0.0).astype(BF16)], axis=1)
    v_aug = jnp.concatenate(
        [v_ref[...].astype(BF16), jnp.where(key_lane == 0, 1.0, 0.0).astype(BF16)], axis=1)
    eye = (lax.broadcasted_iota(jnp.int32, (nbp, LANES), 0)
           == lax.broadcasted_iota(jnp.int32, (nbp, LANES), 1)).astype(BF16)
    group = q_ref.shape[1] // ATT_HEAD
    rows = group * bs
    blk = lax.broadcasted_iota(jnp.int32, (nbp, rows), 0)
    qi = lax.broadcasted_iota(jnp.int32, (rows, bs), 0) % bs
    ki = lax.broadcasted_iota(jnp.int32, (rows, bs), 1)
    causal_bias = jnp.where(ki <= qi, 0.0, MASKED_LOGIT)
    for qb in range(nb):
        own = slice(qb * bs, (qb + 1) * bs)
        q = jnp.concatenate([q_ref[own, g * ATT_HEAD:(g + 1) * ATT_HEAD] for g in range(group)],
                            axis=0)
        qs = (q * qscale).astype(BF16)
        l_own = _dot_nt(qs, k_aug[own, :ATT_HEAD]) + causal_bias
        m = jnp.max(l_own, axis=-1, keepdims=True)
        if qb > 0:
            if qb > MOBA_TOPK:
                q1, q2, _ = _split3(q)
                gate = _dot_nt(km1, q1) + _dot_nt(km1, q2) + _dot_nt(km2, q1)
                cnt = jnp.zeros((nbp, rows), jnp.int32)
                for mb in range(qb):
                    gm = gate[mb:mb + 1, :]
                    beats = (gm > gate) | ((gm == gate) & (mb < blk))
                    cnt = cnt + beats.astype(jnp.int32)
                notsel = jnp.where((cnt >= MOBA_TOPK) & (blk < qb), 1.0, 0.0)
                flags = _dot_tn(notsel, eye).astype(BF16)
            else:
                flags = jnp.zeros((rows, LANES), BF16)
            l_past = _dot_nt(jnp.concatenate([qs, flags], axis=1), k_aug[:qb * bs])
            m = jnp.maximum(m, jnp.max(l_past, axis=-1, keepdims=True))
            acc = (_dot(jnp.exp2(l_past - m), v_aug[:qb * bs]) + _dot(jnp.exp2(l_own - m), v_aug[own]))
        else:
            acc = _dot(jnp.exp2(l_own - m), v_aug[own])
        out = (acc[:, :ATT_HEAD] / acc[:, ATT_HEAD:ATT_HEAD + 1]).astype(BF16)
        for g in range(group):
            o_ref[own, g * ATT_HEAD:(g + 1) * ATT_HEAD] = out[g * bs:(g + 1) * bs]


def _moba(q, kv, batch):
    n, dq = q.shape
    t = n // batch
    gw = dq // KV_HEADS
    return pl.pallas_call(
        _moba_kernel, grid=(batch, KV_HEADS),
        in_specs=[pl.BlockSpec((t, gw), lambda b, kh: (b, kh)),
                  pl.BlockSpec((t, ATT_HEAD), lambda b, kh: (b, kh)),
                  pl.BlockSpec((t, ATT_HEAD), lambda b, kh: (b, KV_HEADS + kh))],
        out_specs=pl.BlockSpec((t, gw), lambda b, kh: (b, kh)),
        out_shape=jax.ShapeDtypeStruct((n, dq), BF16),
        compiler_params=_cparams(("parallel", "parallel")))(q, kv, kv)


def _pad_lora(w_in, w_out):
    r = w_in.shape[1]
    rp = -(-r // LANES) * LANES
    return (jnp.pad(w_in, ((0, 0), (0, rp - r))).astype(BF16),
            jnp.pad(w_out, ((0, rp - r), (0, 0))).astype(BF16))


def _rope_tables(t):
    half = ATT_HEAD // 2
    inv = ROPE_THETA ** (-jnp.arange(half, dtype=F32) / half)
    ang = jnp.arange(t, dtype=F32)[:, None] * inv[None, :]
    cos, sin = jnp.cos(ang), jnp.sin(ang)
    return jnp.concatenate([cos, cos], axis=-1), jnp.concatenate([-sin, sin], axis=-1)


def kernel(x, ln_mix_g, ln_ffn_g, w_ff1, w_ff2, rw_mu, rw_w_rkv, rw_w0, rw_w1, rw_w2, rw_a0, rw_a1, rw_a2, rw_g1, rw_g2, rw_k_k, rw_k_a, rw_r_k, rw_gn_w, rw_gn_b, rw_w_o, rw_v0, rw_v1, rw_v2, kv_norm_g, w_kv, mb_w_q, mb_w_o, final_g):
    batch, t, d = x.shape
    n = batch * t
    depth = ln_mix_g.shape[0]
    n_rwkv = rw_mu.shape[0]
    h = x.reshape(n, d)
    cos, sin = _rope_tables(t)
    rkv_f32 = rw_w_rkv.reshape(n_rwkv, 3 * d, d)
    v_first = None
    kv = None
    for layer in range(depth):
        if layer < n_rwkv:
            i = layer
            w1, w2 = _pad_lora(rw_w1[i], rw_w2[i])
            a1, a2 = _pad_lora(rw_a1[i], rw_a2[i])
            g1, g2 = _pad_lora(rw_g1[i], rw_g2[i])
            v_lora = None
            if i > 0:
                v1, v2 = _pad_lora(rw_v1[i - 1], rw_v2[i - 1])
                v_lora = (rw_v0[i - 1], v1, v2)
            first = layer == 0
            prep = _rwkv_prep(h, t, ln_mix_g[layer], rw_mu[i], rw_w0[i], w1, w2,
                              rw_a0[i], a1, a2, g1, g2, v_lora,
                              [(rkv_f32, 0)] if first else [])
            x3, lw, a, gate = prep[:4]
            v_gate = prep[4] if i > 0 else None
            if first:
                rkv_b = prep[-1]
            rkv = _mm3(x3, rkv_b.reshape(1, 3, d, d), 0)
            res = _wkv(rkv, lw, a, gate, v_first, v_gate, rw_k_k[i], rw_k_a[i], rw_r_k[i],
                       rw_gn_w[i], rw_gn_b[i], batch,
                       [(w_ff1, 0), (w_ff2, 0), (rw_w_o, 0)] if first else [])
            yg = res[0]
            if first:
                w1_b, w2_b, wo_b = res[1:]
            if i == 0:
                v_first = rkv
            h = _mm_res(yg, wo_b, 0, h)
        else:
            if kv is None:
                kv = _norm_mm_rope(h, kv_norm_g, wkv_b, 0, cos, sin, KV_HEADS * ATT_HEAD, t)
            q = _norm_mm_rope(h, ln_mix_g[layer], wq_b, 0, cos, sin, mb_w_q.shape[2], t,
                              tm=512, tn=mb_w_q.shape[2])
            att = _moba(q, kv, batch)
            h = _mm_res(att, wo_b, 0, h)
        nxt = layer + 1
        jobs = []
        if nxt < depth:
            jobs = [(w_ff1, nxt), (w_ff2, nxt)]
            if nxt < n_rwkv:
                jobs += [(rkv_f32, nxt), (rw_w_o, nxt)]
            else:
                jobs += [(mb_w_q, nxt - n_rwkv), (mb_w_o, nxt - n_rwkv)]
                if nxt == n_rwkv:
                    jobs.append((w_kv[None], 0))
        res = _mlp(h, ln_ffn_g[layer], w1_b, w2_b, 0, final_g, nxt == depth, jobs)
        h = res[0]
        if jobs:
            w1_b, w2_b = res[1:3]
            if nxt < n_rwkv:
                rkv_b, wo_b = res[3:5]
            else:
                wq_b, wo_b = res[3:5]
                if nxt == n_rwkv:
                    wkv_b = res[5]
    return h.reshape(batch, t, d)
```
